```python
import math, functools
import jax, jax.numpy as jnp
from jax import lax
import numpy as np

D_MODEL = 1024
BATCH = 2
SEQ = 8192
DEPTH = 1
DEC_BATCH = 32
DEC_SEQ = 1
PAST_LEN = 16384
PAGE_SIZE = 128

N_HEADS = 8
HEAD_DIM = 64
V_DIM = 2 * HEAD_DIM
QK_WIDTH = N_HEADS * 2 * HEAD_DIM
ATTN_WIDTH = N_HEADS * V_DIM
ROPE_DIM = HEAD_DIM // 4
ROPE_THETA = 500000.0
Q_BLOCK = 128
SSM_EXPAND = 2
D_INNER = SSM_EXPAND * D_MODEL
SSM_HEAD_DIM = 64
SSM_HEADS = D_INNER // SSM_HEAD_DIM
N_GROUPS = 4
HEADS_PER_GROUP = SSM_HEADS // N_GROUPS
D_STATE = 128
CONV_W = 4
CONV_DIM = D_INNER + 2 * N_GROUPS * D_STATE
CHUNK = 128
D_FF = -(-8 * D_MODEL // (3 * 256)) * 256
N_BRANCHES = 2
GATE_WIDTH = N_BRANCHES * D_MODEL
IN_WIDTH = 2 * QK_WIDTH + ATTN_WIDTH + D_INNER + CONV_DIM + SSM_HEADS + GATE_WIDTH
IN_SPLITS = tuple(int(s) for s in np.cumsum([QK_WIDTH, QK_WIDTH, ATTN_WIDTH, D_INNER, CONV_DIM, SSM_HEADS]))
EPS = 1e-6
F32 = jnp.float32

kernel_name = 'cond_hybrid_diffattn_mamba2_decode_step'


def rms_norm(x, w):
    xf = x.astype(F32)
    y = xf * lax.rsqrt(jnp.mean(xf * xf, axis=-1, keepdims=True) + EPS)
    return (y * w.astype(F32)).astype(x.dtype)


def rope_partial(t, pos):
    half = ROPE_DIM // 2
    inv_freq = ROPE_THETA ** (-jnp.arange(half, dtype=F32) * 2.0 / ROPE_DIM)
    ang = pos.astype(F32)[:, None] * inv_freq[None, :]
    cos = jnp.cos(ang)[:, None, None, :]
    sin = jnp.sin(ang)[:, None, None, :]
    tr = t[..., :ROPE_DIM].astype(F32)
    t1, t2 = tr[..., :half], tr[..., half:]
    rot = jnp.concatenate([t1 * cos - t2 * sin, t2 * cos + t1 * sin], axis=-1)
    return jnp.concatenate([rot.astype(t.dtype), t[..., ROPE_DIM:]], axis=-1)


def diff_attn_core(q, k, v, q_pos, k_pos, lam):
    s = jnp.einsum('bqhmd,bkhmd->bhmqk', q, k, preferred_element_type=F32) / math.sqrt(HEAD_DIM)
    mask = k_pos[None, :] <= q_pos[:, None]
    s = jnp.where(mask, s, -jnp.inf)
    pr = jax.nn.softmax(s, axis=-1)
    w = pr[:, :, 0] - lam * pr[:, :, 1]
    return jnp.einsum('bhqk,bkhe->bqhe', w.astype(v.dtype), v)


def prompt_attention(q, k, v, lam):
    b, L = q.shape[:2]
    nb = L // Q_BLOCK
    qb = jnp.swapaxes(q.reshape(b, nb, Q_BLOCK, N_HEADS, 2, HEAD_DIM), 0, 1)
    k_pos = jnp.arange(L)

    def one_block(args):
        qi, i = args
        q_pos = i * Q_BLOCK + jnp.arange(Q_BLOCK)
        return diff_attn_core(qi, k, v, q_pos, k_pos, lam)

    o = lax.map(one_block, (qb, jnp.arange(nb)))
    return jnp.swapaxes(o, 0, 1).reshape(b, L, N_HEADS, V_DIM)


def sample_attention(q, k, v, lam, cache_k, cache_v, page_table):
    n_pages = page_table.shape[1]
    past = n_pages * PAGE_SIZE
    Ld = q.shape[1]
    q_pos = past + jnp.arange(Ld)
    k_pos = jnp.arange(past + Ld)

    def one_seq(args):
        qi, kn, vn, pt = args
        kp = cache_k[pt].reshape(past, N_HEADS, 2, HEAD_DIM).astype(kn.dtype)
        vp = cache_v[pt].reshape(past, N_HEADS, V_DIM).astype(vn.dtype)
        kk = jnp.concatenate([kp, kn], axis=0)[None]
        vv = jnp.concatenate([vp, vn], axis=0)[None]
        return diff_attn_core(qi[None], kk, vv, q_pos, k_pos, lam)[0]

    return lax.map(one_seq, (q, k, v, page_table))


def causal_conv_silu(ext, w, b, L):
    acc = b.astype(F32)
    for j in range(CONV_W):
        acc = acc + ext[:, j:j + L].astype(F32) * w[j].astype(F32)
    return jax.nn.silu(acc)


def ssd_chunked(x, dt, A, B, C, s0):
    b, L = x.shape[:2]
    nc = L // CHUNK
    rs = lambda t: t.reshape((b, nc, CHUNK) + t.shape[2:])
    x, dt, B, C = rs(x), rs(dt), rs(B), rs(C)
    a_cum = jnp.cumsum(dt * A, axis=2)
    seg = a_cum[:, :, :, None] - a_cum[:, :, None, :]
    causal = jnp.tril(jnp.ones((CHUNK, CHUNK), bool))[:, :, None, None]
    decay = jnp.exp(jnp.where(causal, seg, -jnp.inf))
    cb = jnp.einsum('bcign,bcjgn->bcijg', C, B)
    wgt = cb[..., None] * decay * dt[:, :, None]
    y_diag = jnp.einsum('bcijgr,bcjgrp->bcigrp', wgt, x)
    decay_end = jnp.exp(a_cum[:, :, -1:] - a_cum)
    chunk_states = jnp.einsum('bcjgn,bcjgr,bcjgrp->bcgrpn', B, decay_end * dt, x)
    chunk_decay = jnp.exp(a_cum[:, :, -1])

    def step(s, inp):
        cs, cd = inp
        return cd[..., None, None] * s + cs, s

    s_final, s_in = lax.scan(step, s0, (jnp.swapaxes(chunk_states, 0, 1), jnp.swapaxes(chunk_decay, 0, 1)))
    s_in = jnp.swapaxes(s_in, 0, 1)
    y_off = jnp.einsum('bcign,bcigr,bcgrpn->bcigrp', C, jnp.exp(a_cum), s_in)
    y = (y_diag + y_off).reshape((b, L) + x.shape[3:])
    return y, s_final


def ssd_recurrent(x, dt, A, B, C, s0):
    def step(s, inp):
        xt, dtt, Bt, Ct = inp
        s = jnp.exp(dtt * A)[..., None, None] * s + jnp.einsum('bgr,bgrp,bgn->bgrpn', dtt, xt, Bt)
        return s, jnp.einsum('bgn,bgrpn->bgrp', Ct, s)

    s_final, ys = lax.scan(step, s0, (jnp.swapaxes(x, 0, 1), jnp.swapaxes(dt, 0, 1), jnp.swapaxes(B, 0, 1), jnp.swapaxes(C, 0, 1)))
    return jnp.swapaxes(ys, 0, 1), s_final


def mamba_branch(z, xbc, dt_raw, conv_prev, ssm_prev, p, recurrent):
    b, L = z.shape[:2]
    ext = jnp.concatenate([conv_prev.astype(xbc.dtype), xbc], axis=1)
    new_conv = ext[:, ext.shape[1] - (CONV_W - 1):]
    u = causal_conv_silu(ext, p['conv_w'], p['conv_b'], L)
    xs = u[..., :D_INNER].reshape(b, L, N_GROUPS, HEADS_PER_GROUP, SSM_HEAD_DIM)
    Bm = u[..., D_INNER:D_INNER + N_GROUPS * D_STATE].reshape(b, L, N_GROUPS, D_STATE)
    Cm = u[..., D_INNER + N_GROUPS * D_STATE:].reshape(b, L, N_GROUPS, D_STATE)
    dt = jax.nn.softplus(dt_raw.astype(F32) + p['dt_bias'].astype(F32)).reshape(b, L, N_GROUPS, HEADS_PER_GROUP)
    A = -jnp.exp(p['a_log'].astype(F32)).reshape(N_GROUPS, HEADS_PER_GROUP)
    s0 = ssm_prev.astype(F32).reshape(b, N_GROUPS, HEADS_PER_GROUP, SSM_HEAD_DIM, D_STATE)
    scan_fn = ssd_recurrent if recurrent else ssd_chunked
    y, s_final = scan_fn(xs, dt, A, Bm, Cm, s0)
    y = y + p['d_skip'].astype(F32).reshape(N_GROUPS, HEADS_PER_GROUP)[:, :, None] * xs
    y = y.reshape(b, L, D_INNER) * jax.nn.silu(z.astype(F32))
    yg = y.reshape(b, L, N_GROUPS, D_INNER // N_GROUPS)
    yg = yg * lax.rsqrt(jnp.mean(yg * yg, axis=-1, keepdims=True) + EPS)
    y = yg.reshape(b, L, D_INNER) * p['ssm_norm_w'].astype(F32)
    return y.astype(z.dtype), new_conv, s_final.reshape(b, SSM_HEADS, SSM_HEAD_DIM, D_STATE).astype(z.dtype)


def decoder_layer(x, c, pos, attend, conv_prev, ssm_prev, recurrent, p, lam_init):
    b, L, _ = x.shape
    mod = jnp.einsum('bd,de->be', jax.nn.silu(c), p['w_ada']) + p['b_ada']
    sh1, sc1, g1, sh2, sc2, g2 = jnp.split(mod[:, None, :], 6, axis=-1)
    h = rms_norm(x, p['norm1_w']) * (1.0 + sc1) + sh1
    proj = jnp.einsum('bld,de->ble', h, p['w_in'])
    q, k, v, z, xbc, dt_raw, gate_logits = jnp.split(proj, IN_SPLITS, axis=-1)
    q = rope_partial(q.reshape(b, L, N_HEADS, 2, HEAD_DIM), pos)
    k = rope_partial(k.reshape(b, L, N_HEADS, 2, HEAD_DIM), pos)
    v = v.reshape(b, L, N_HEADS, V_DIM)
    lam = (jnp.exp(jnp.sum(p['lambda_q1'].astype(F32) * p['lambda_k1'].astype(F32)))
           - jnp.exp(jnp.sum(p['lambda_q2'].astype(F32) * p['lambda_k2'].astype(F32))) + lam_init)
    o = attend(q, k, v, lam)
    o = (rms_norm(o, p['subln_w']) * (1.0 - lam_init)).reshape(b, L, ATTN_WIDTH)
    ys, new_conv, new_ssm = mamba_branch(z, xbc, dt_raw, conv_prev, ssm_prev, p, recurrent)
    gates = jax.nn.sigmoid(gate_logits.astype(F32)).astype(x.dtype)
    ga, gb = gates[..., :D_MODEL], gates[..., D_MODEL:]
    ya = jnp.einsum('ble,ed->bld', o, p['w_branch_attn'])
    yb = jnp.einsum('ble,ed->bld', ys, p['w_branch_ssm'])
    mixed = jnp.einsum('bld,de->ble', ga * ya + gb * yb, p['w_out'])
    x = x + g1 * mixed
    h2 = rms_norm(x, p['norm2_w']) * (1.0 + sc2) + sh2
    ff = jax.nn.silu(h2 @ p['w_ffn_gate']) * (h2 @ p['w_ffn_up'])
    x = x + g2 * (ff @ p['w_ffn_down'])
    return x, k.reshape(b, L, N_HEADS, 2 * HEAD_DIM), v, new_conv, new_ssm


def setup_inputs(seed: int = 0) -> dict:
    key = jax.random.key(seed)
    ks = iter(jax.random.split(key, 40))

    def nrm(shape, scale):
        return jax.random.normal(next(ks), shape, F32) * scale

    n_pages = PAST_LEN // PAGE_SIZE
    n_pool = (5 * DEC_BATCH * n_pages + 3) // 4
    Ly = DEPTH
    d = D_MODEL
    x_prompt = nrm((BATCH, SEQ, d), 1.0)
    x_sample = nrm((DEC_BATCH, DEC_SEQ, d), 1.0)
    c_prompt = nrm((BATCH, d), 1.0)
    c_sample = nrm((DEC_BATCH, d), 1.0)
    cache_k = nrm((Ly, n_pool, PAGE_SIZE, N_HEADS, 2 * HEAD_DIM), 1.0)
    cache_v = nrm((Ly, n_pool, PAGE_SIZE, N_HEADS, 2 * HEAD_DIM), 1.0)
    page_table = jax.random.permutation(next(ks), n_pool)[:DEC_BATCH * n_pages].reshape(DEC_BATCH, n_pages).astype(jnp.int32)
    state_conv = nrm((Ly, DEC_BATCH, CONV_W - 1, CONV_DIM), 1.0)
    state_ssm = nrm((Ly, DEC_BATCH, SSM_HEADS, SSM_HEAD_DIM, D_STATE), 0.1)
    w_ada = nrm((Ly, d, 6 * d), 0.5 * d ** -0.5)
    b_ada = nrm((Ly, 6 * d), 0.01)
    norm1_w = 1.0 + nrm((Ly, d), 0.02)
    w_in = nrm((Ly, d, IN_WIDTH), d ** -0.5)
    lambda_q1 = nrm((Ly, HEAD_DIM), 0.1)
    lambda_k1 = nrm((Ly, HEAD_DIM), 0.1)
    lambda_q2 = nrm((Ly, HEAD_DIM), 0.1)
    lambda_k2 = nrm((Ly, HEAD_DIM), 0.1)
    subln_w = 1.0 + nrm((Ly, V_DIM), 0.02)
    conv_w = nrm((Ly, CONV_W, CONV_DIM), CONV_W ** -0.5)
    conv_b = nrm((Ly, CONV_DIM), 0.01)
    dt0 = jnp.exp(jax.random.uniform(next(ks), (Ly, SSM_HEADS), F32, math.log(1e-3), math.log(1e-1)))
    dt_bias = dt0 + jnp.log(-jnp.expm1(-dt0))
    a_log = jnp.log(jax.random.uniform(next(ks), (Ly, SSM_HEADS), F32, 1.0, 16.0))
    d_skip = 1.0 + nrm((Ly, SSM_HEADS), 0.1)
    ssm_norm_w = 1.0 + nrm((Ly, D_INNER), 0.02)
    w_branch_attn = nrm((Ly, ATTN_WIDTH, d), ATTN_WIDTH ** -0.5)
    w_branch_ssm = nrm((Ly, D_INNER, d), D_INNER ** -0.5)
    w_out = nrm((Ly, d, d), d ** -0.5)
    norm2_w = 1.0 + nrm((Ly, d), 0.02)
    w_ffn_gate = nrm((Ly, d, D_FF), d ** -0.5)
    w_ffn_up = nrm((Ly, d, D_FF), d ** -0.5)
    w_ffn_down = nrm((Ly, D_FF, d), D_FF ** -0.5)
    final_norm_w = 1.0 + nrm((d,), 0.02)
    return {'x_prompt': x_prompt, 'x_sample': x_sample, 'c_prompt': c_prompt, 'c_sample': c_sample,
            'cache_k': cache_k, 'cache_v': cache_v, 'page_table': page_table,
            'state_conv': state_conv, 'state_ssm': state_ssm,
            'w_ada': w_ada, 'b_ada': b_ada, 'norm1_w': norm1_w, 'w_in': w_in,
            'lambda_q1': lambda_q1, 'lambda_k1': lambda_k1, 'lambda_q2': lambda_q2, 'lambda_k2': lambda_k2,
            'subln_w': subln_w, 'conv_w': conv_w, 'conv_b': conv_b, 'dt_bias': dt_bias, 'a_log': a_log,
            'd_skip': d_skip, 'ssm_norm_w': ssm_norm_w, 'w_branch_attn': w_branch_attn,
            'w_branch_ssm': w_branch_ssm, 'w_out': w_out, 'norm2_w': norm2_w,
            'w_ffn_gate': w_ffn_gate, 'w_ffn_up': w_ffn_up, 'w_ffn_down': w_ffn_down,
            'final_norm_w': final_norm_w}


def reference(x_prompt, x_sample, c_prompt, c_sample, cache_k, cache_v, page_table, state_conv, state_ssm,
              w_ada, b_ada, norm1_w, w_in, lambda_q1, lambda_k1, lambda_q2, lambda_k2, subln_w,
              conv_w, conv_b, dt_bias, a_log, d_skip, ssm_norm_w, w_branch_attn, w_branch_ssm, w_out,
              norm2_w, w_ffn_gate, w_ffn_up, w_ffn_down, final_norm_w):
    bp = x_prompt.shape[0]
    pos_p = jnp.arange(x_prompt.shape[1])
    past = page_table.shape[1] * PAGE_SIZE
    pos_s = past + jnp.arange(x_sample.shape[1])
    hp, hs = x_prompt, x_sample
    kp_l, vp_l, cp_l, sp_l, ks_l, vs_l, cs_l, ss_l = [], [], [], [], [], [], [], []
    for i in range(DEPTH):
        p = {'w_ada': w_ada[i], 'b_ada': b_ada[i], 'norm1_w': norm1_w[i], 'w_in': w_in[i],
             'lambda_q1': lambda_q1[i], 'lambda_k1': lambda_k1[i], 'lambda_q2': lambda_q2[i],
             'lambda_k2': lambda_k2[i], 'subln_w': subln_w[i], 'conv_w': conv_w[i], 'conv_b': conv_b[i],
             'dt_bias': dt_bias[i], 'a_log': a_log[i], 'd_skip': d_skip[i], 'ssm_norm_w': ssm_norm_w[i],
             'w_branch_attn': w_branch_attn[i], 'w_branch_ssm': w_branch_ssm[i], 'w_out': w_out[i],
             'norm2_w': norm2_w[i], 'w_ffn_gate': w_ffn_gate[i], 'w_ffn_up': w_ffn_up[i],
             'w_ffn_down': w_ffn_down[i]}
        lam_init = 0.8 - 0.6 * math.exp(-0.3 * i)
        conv0 = jnp.zeros((bp, CONV_W - 1, CONV_DIM), x_prompt.dtype)
        ssm0 = jnp.zeros((bp, SSM_HEADS, SSM_HEAD_DIM, D_STATE), x_prompt.dtype)
        hp, kp, vp, cp, sp = decoder_layer(hp, c_prompt, pos_p, prompt_attention, conv0, ssm0, False, p, lam_init)
        attend_s = functools.partial(sample_attention, cache_k=cache_k[i], cache_v=cache_v[i], page_table=page_table)
        hs, ks, vs, cs, ss = decoder_layer(hs, c_sample, pos_s, attend_s, state_conv[i], state_ssm[i], True, p, lam_init)
        kp_l.append(kp); vp_l.append(vp); cp_l.append(cp); sp_l.append(sp)
        ks_l.append(ks); vs_l.append(vs); cs_l.append(cs); ss_l.append(ss)
    y_prompt = rms_norm(hp, final_norm_w)
    y_sample = rms_norm(hs, final_norm_w)
    return (y_prompt, y_sample, jnp.stack(kp_l), jnp.stack(vp_l), jnp.stack(cp_l), jnp.stack(sp_l),
            jnp.stack(ks_l), jnp.stack(vs_l), jnp.stack(cs_l), jnp.stack(ss_l))
```

```python
import functools
import math

import jax
import jax.numpy as jnp
import numpy as np
from jax import lax
from jax.experimental import pallas as pl
from jax.experimental.pallas import tpu as pltpu

F32 = jnp.float32
BF16 = jnp.bfloat16

D_MODEL = 1024
N_HEADS = 8
HEAD_DIM = 64
V_DIM = 2 * HEAD_DIM
QK_WIDTH = N_HEADS * 2 * HEAD_DIM
ATTN_WIDTH = N_HEADS * V_DIM
ROPE_DIM = HEAD_DIM // 4
ROPE_THETA = 500000.0
D_INNER = 2 * D_MODEL
SSM_HEAD_DIM = 64
SSM_HEADS = D_INNER // SSM_HEAD_DIM
N_GROUPS = 4
HEADS_PER_GROUP = SSM_HEADS // N_GROUPS
D_STATE = 128
CONV_W = 4
CONV_DIM = D_INNER + 2 * N_GROUPS * D_STATE
D_FF = -(-8 * D_MODEL // (3 * 256)) * 256
GATE_WIDTH = 2 * D_MODEL
PAGE_SIZE = 128
EPS = 1e-6

LANES = 128
SUBLANES = 8
DT_PAD = LANES
IN_WIDTH_PAD = 2 * QK_WIDTH + ATTN_WIDTH + D_INNER + CONV_DIM + GATE_WIDTH + DT_PAD
VMEM_LIMIT = 56 * 1024 * 1024

_OQ = 0
_OK = _OQ + QK_WIDTH
_OV = _OK + QK_WIDTH
_OZ = _OV + ATTN_WIDTH
_OX = _OZ + D_INNER
_OG = _OX + CONV_DIM
_OD = _OG + GATE_WIDTH


def _silu(x):
    return x * (1.0 / (1.0 + jnp.exp(-x)))


def _sigmoid(x):
    return 1.0 / (1.0 + jnp.exp(-x))


def _softplus(x):
    return jnp.maximum(x, 0.0) + jnp.log(1.0 + jnp.exp(-jnp.abs(x)))


def _split3(a):
    hi = a.astype(BF16)
    r1 = a - hi.astype(F32)
    mid = r1.astype(BF16)
    lo = (r1 - mid.astype(F32)).astype(BF16)
    return hi, mid, lo


def _dot(a, b):
    return jnp.dot(a, b, preferred_element_type=F32)


def _dot_nt(a, b):
    return lax.dot_general(a, b, (((1,), (1,)), ((), ())), preferred_element_type=F32)


def _exact_right(a, e01):
    hi, mid, lo = _split3(a)
    return _dot(hi, e01) + _dot(mid, e01) + _dot(lo, e01)


def _exact_left(t01, a):
    hi, mid, lo = _split3(a)
    return _dot(t01, hi) + _dot(t01, mid) + _dot(t01, lo)


def _ada_kernel(c_ref, w_ref, b_ref, o_ref):
    c = _silu(c_ref[...]).astype(BF16)
    o_ref[...] = _dot(c, w_ref[...].astype(BF16)) + b_ref[...]


def _ada_mod(c_all, w_ada, b_ada):
    rows = c_all.shape[0]
    tn = D_MODEL
    return pl.pallas_call(
        _ada_kernel,
        grid=(6 * D_MODEL // tn,),
        in_specs=[pl.BlockSpec((rows, D_MODEL), lambda j: (0, 0)),
                  pl.BlockSpec((D_MODEL, tn), lambda j: (0, j)),
                  pl.BlockSpec((1, tn), lambda j: (0, j))],
        out_specs=pl.BlockSpec((rows, tn), lambda j: (0, j)),
        out_shape=jax.ShapeDtypeStruct((rows, 6 * D_MODEL), F32),
        name="ada_mod",
    )(c_all, w_ada, b_ada.reshape(1, -1))


def _rope_cols(t, c_tab, s1_tab, s2_tab, scale):
    outs = []
    for cb in range(t.shape[1] // LANES):
        tc = t[:, cb * LANES:(cb + 1) * LANES]
        up = pltpu.roll(tc, LANES - ROPE_DIM // 2, axis=1)
        dn = pltpu.roll(tc, ROPE_DIM // 2, axis=1)
        r = tc * c_tab + up * s1_tab + dn * s2_tab
        outs.append(r * scale if scale != 1.0 else r)
    return outs


def _inproj_kernel(x_ref, sh_ref, sc_ref, nw_ref, w_ref, ct_ref, s1_ref, s2_ref,
                   q_ref, k_ref, kb_ref, v_ref, vb_ref, z_ref, xbc_ref, g_ref, dt_ref):
    x = x_ref[...]
    ms = jnp.mean(x * x, axis=-1, keepdims=True)
    h = x * lax.rsqrt(ms + EPS) * nw_ref[...]
    h = h * (1.0 + sc_ref[0]) + sh_ref[0]
    hb = h.astype(BF16)

    def proj(lo, width):
        return _dot(hb, w_ref[:, lo:lo + width])

    ct, s1, s2 = ct_ref[...], s1_ref[...], s2_ref[...]
    q = _rope_cols(proj(_OQ, QK_WIDTH), ct, s1, s2, 1.0 / math.sqrt(HEAD_DIM))
    for cb, r in enumerate(q):
        q_ref[:, cb * LANES:(cb + 1) * LANES] = r.astype(BF16)
    k = _rope_cols(proj(_OK, QK_WIDTH), ct, s1, s2, 1.0)
    for cb, r in enumerate(k):
        k_ref[:, cb * LANES:(cb + 1) * LANES] = r
        kb_ref[:, cb * LANES:(cb + 1) * LANES] = r.astype(BF16)
    v = proj(_OV, ATTN_WIDTH)
    v_ref[...] = v
    vb_ref[...] = v.astype(BF16)
    z_ref[...] = proj(_OZ, D_INNER)
    xbc_ref[...] = proj(_OX, CONV_DIM)
    g_ref[...] = proj(_OG, GATE_WIDTH)
    dt_ref[...] = proj(_OD, DT_PAD)


def _in_proj(x2d, sh, sc, nw, w_in, ctab, s1tab, s2tab, tm, rows_per_mod):
    rows = x2d.shape[0]
    mod_rows = sh.shape[1]
    tiles_per_mod = rows_per_mod // tm
    row_blk = lambda width: pl.BlockSpec((tm, width), lambda i: (i, 0))
    mod_blk = pl.BlockSpec((1, mod_rows, D_MODEL), lambda i: (i // tiles_per_mod, 0, 0))
    assert ctab.shape[0] == rows_per_mod
    tab_blk = pl.BlockSpec((tm, LANES), lambda i: (i % tiles_per_mod, 0))
    outs = [(QK_WIDTH, BF16), (QK_WIDTH, F32), (QK_WIDTH, BF16), (ATTN_WIDTH, F32), (ATTN_WIDTH, BF16),
            (D_INNER, F32), (CONV_DIM, F32), (GATE_WIDTH, F32), (DT_PAD, F32)]
    return pl.pallas_call(
        _inproj_kernel,
        grid=(rows // tm,),
        in_specs=[row_blk(D_MODEL), mod_blk, mod_blk,
                  pl.BlockSpec((1, D_MODEL), lambda i: (0, 0)),
                  pl.BlockSpec((D_MODEL, IN_WIDTH_PAD), lambda i: (0, 0), pipeline_mode=pl.Buffered(1)),
                  tab_blk, tab_blk, tab_blk],
        out_specs=[row_blk(w) for w, _ in outs],
        out_shape=[jax.ShapeDtypeStruct((rows, w), dt) for w, dt in outs],
        compiler_params=pltpu.CompilerParams(dimension_semantics=("arbitrary",), vmem_limit_bytes=VMEM_LIMIT),
        name="in_proj",
    )(x2d, sh, sc, nw, w_in, ctab, s1tab, s2tab)


def _rope_tables(pos):
    half = ROPE_DIM // 2
    inv_freq = ROPE_THETA ** (-jnp.arange(half, dtype=F32) * 2.0 / ROPE_DIM)
    ang = pos.astype(F32)[:, None] * inv_freq[None, :]
    cos, sin = jnp.cos(ang), jnp.sin(ang)
    n = pos.shape[0]
    rest = HEAD_DIM - ROPE_DIM
    c64 = jnp.concatenate([cos, cos, jnp.ones((n, rest), F32)], axis=1)
    s1_64 = jnp.concatenate([-sin, jnp.zeros((n, half + rest), F32)], axis=1)
    s2_64 = jnp.concatenate([jnp.zeros((n, half), F32), sin, jnp.zeros((n, rest), F32)], axis=1)
    rep = LANES // HEAD_DIM
    return jnp.tile(c64, (1, rep)), jnp.tile(s1_64, (1, rep)), jnp.tile(s2_64, (1, rep))


def _lambda_value(lq1, lk1, lq2, lk2, lam_init):
    a = jnp.sum(lq1[...] * lk1[...], axis=1, keepdims=True)
    b = jnp.sum(lq2[...] * lk2[...], axis=1, keepdims=True)
    return jnp.exp(a) - jnp.exp(b) + lam_init


def _flash_update(s, v, m_ref, l_ref, acc_ref, idx):
    m_prev = m_ref[idx]
    m_new = jnp.maximum(m_prev, jnp.max(s, axis=1, keepdims=True))
    alpha = jnp.exp(m_prev - m_new)
    p = jnp.exp(s - m_new)
    l_ref[idx] = alpha * l_ref[idx] + jnp.sum(p, axis=1, keepdims=True)
    acc_ref[idx] = alpha * acc_ref[idx] + _dot(p.astype(BF16), v)
    m_ref[idx] = m_new


def _diff_finish(m_ref, l_ref, acc_ref, lam, subln_w, lam_init):
    o1 = acc_ref[0] / l_ref[0]
    o2 = acc_ref[1] / l_ref[1]
    o = o1 - lam * o2
    ms = jnp.mean(o * o, axis=-1, keepdims=True)
    return o * lax.rsqrt(ms + EPS) * subln_w * (1.0 - lam_init)


def _prompt_attn_kernel(q_ref, k_ref, v_ref, lq1, lk1, lq2, lk2, sw_ref, o_ref,
                        qm_ref, m_ref, l_ref, acc_ref, *, tq, lam_init):
    qi = pl.program_id(2)
    q = q_ref[0]
    lane = lax.broadcasted_iota(jnp.int32, q.shape, 1)
    zero = jnp.zeros_like(q)
    qm_ref[0] = jnp.where(lane < HEAD_DIM, q, zero)
    qm_ref[1] = jnp.where(lane >= HEAD_DIM, q, zero)
    m_ref[...] = jnp.full(m_ref.shape, -jnp.inf, F32)
    l_ref[...] = jnp.zeros(l_ref.shape, F32)
    acc_ref[...] = jnp.zeros(acc_ref.shape, F32)

    def tile(start, masked):
        k = k_ref[0, pl.ds(start, tq), :]
        v = v_ref[0, pl.ds(start, tq), :]
        for mp in range(2):
            s = _dot_nt(qm_ref[mp], k)
            if masked:
                row = lax.broadcasted_iota(jnp.int32, s.shape, 0)
                col = lax.broadcasted_iota(jnp.int32, s.shape, 1)
                s = jnp.where(col <= row, s, -jnp.inf)
            _flash_update(s, v, m_ref, l_ref, acc_ref, mp)

    def body(ki, carry):
        tile(pl.multiple_of(ki * tq, tq), False)
        return carry

    lax.fori_loop(0, qi, body, 0)
    tile(pl.multiple_of(qi * tq, tq), True)
    lam = _lambda_value(lq1, lk1, lq2, lk2, lam_init)
    o_ref[0] = _diff_finish(m_ref, l_ref, acc_ref, lam, sw_ref[...], lam_init).astype(o_ref.dtype)


def _prompt_attention(qb, kb, vb, lq1, lk1, lq2, lk2, subln_w, lam_init, tq):
    b, L, _ = qb.shape
    vec = pl.BlockSpec((1, HEAD_DIM), lambda bi, h, qi: (0, 0))
    kv_blk = pl.BlockSpec((1, L, V_DIM), lambda bi, h, qi: (bi, 0, h))
    q_blk = pl.BlockSpec((1, tq, V_DIM), lambda bi, h, qi: (bi, qi, h))
    return pl.pallas_call(
        functools.partial(_prompt_attn_kernel, tq=tq, lam_init=lam_init),
        grid=(b, N_HEADS, L // tq),
        in_specs=[q_blk, kv_blk, kv_blk, vec, vec, vec, vec,
                  pl.BlockSpec((1, V_DIM), lambda bi, h, qi: (0, 0))],
        out_specs=q_blk,
        out_shape=jax.ShapeDtypeStruct((b, L, ATTN_WIDTH), BF16),
        scratch_shapes=[pltpu.VMEM((2, tq, V_DIM), BF16),
                        pltpu.VMEM((2, tq, 1), F32),
                        pltpu.VMEM((2, tq, 1), F32),
                        pltpu.VMEM((2, tq, V_DIM), F32)],
        compiler_params=pltpu.CompilerParams(
            dimension_semantics=("arbitrary", "arbitrary", "arbitrary"), vmem_limit_bytes=VMEM_LIMIT),
        name="prompt_attn",
    )(qb, kb, vb, lq1, lk1, lq2, lk2, subln_w)


def _sample_attn_kernel(pt_ref, q_ref, kn_ref, vn_ref, *rest, pages, lam_init):
    k_refs = rest[:pages]
    v_refs = rest[pages:2 * pages]
    lq1, lk1, lq2, lk2, sw_ref, o_ref, m_ref, l_ref, acc_ref = rest[2 * pages:]
    j = pl.program_id(1)
    nrow = 2 * N_HEADS
    q16 = q_ref[0]

    def head_mask(nkeys):
        row = lax.broadcasted_iota(jnp.int32, (nrow, nkeys), 0)
        col = lax.broadcasted_iota(jnp.int32, (nrow, nkeys), 1)
        return (row % N_HEADS) == (col % N_HEADS)

    def update(k2d, v2d, valid):
        s = _dot_nt(q16, k2d)
        s = jnp.where(valid, s, -jnp.inf)
        m_prev = m_ref[...]
        m_new = jnp.maximum(m_prev, jnp.max(s, axis=1, keepdims=True))
        alpha = jnp.exp(m_prev - m_new)
        p = jnp.exp(s - m_new)
        l_ref[...] = alpha * l_ref[...] + jnp.sum(p, axis=1, keepdims=True)
        acc_ref[...] = alpha * acc_ref[...] + _dot(p.astype(BF16), v2d)
        m_ref[...] = m_new

    @pl.when(j == 0)
    def _():
        m_ref[...] = jnp.full(m_ref.shape, -jnp.inf, F32)
        l_ref[...] = jnp.zeros(l_ref.shape, F32)
        acc_ref[...] = jnp.zeros(acc_ref.shape, F32)
        col = lax.broadcasted_iota(jnp.int32, (nrow, LANES), 1)
        update(kn_ref[0].astype(BF16), vn_ref[0].astype(BF16), head_mask(LANES) & (col < N_HEADS))

    valid = head_mask(PAGE_SIZE * N_HEADS)
    for p in range(pages):
        update(k_refs[p][0].astype(BF16), v_refs[p][0].astype(BF16), valid)

    @pl.when(j == pl.num_programs(1) - 1)
    def _():
        lam = _lambda_value(lq1, lk1, lq2, lk2, lam_init)
        o1 = acc_ref[0:N_HEADS, :] / l_ref[0:N_HEADS, :]
        o2 = acc_ref[N_HEADS:nrow, :] / l_ref[N_HEADS:nrow, :]
        o = o1 - lam * o2
        ms = jnp.mean(o * o, axis=-1, keepdims=True)
        o_ref[0] = (o * lax.rsqrt(ms + EPS) * sw_ref[...] * (1.0 - lam_init)).astype(o_ref.dtype)


def _sample_attention(q16, kn_pad, vn_pad, cache_k, cache_v, page_table, lq1, lk1, lq2, lk2, subln_w,
                      lam_init, pages):
    nseq, n_pages = page_table.shape
    steps = n_pages // pages

    def page_blk(i):
        return pl.BlockSpec((1, PAGE_SIZE * N_HEADS, V_DIM), lambda s, j, pt: (pt[s, j * pages + i], 0, 0))

    vec = pl.BlockSpec((1, HEAD_DIM), lambda s, j, pt: (0, 0))
    grid_spec = pltpu.PrefetchScalarGridSpec(
        num_scalar_prefetch=1,
        grid=(nseq, steps),
        in_specs=[pl.BlockSpec((1, 2 * N_HEADS, V_DIM), lambda s, j, pt: (s, 0, 0)),
                  pl.BlockSpec((1, LANES, V_DIM), lambda s, j, pt: (s, 0, 0)),
                  pl.BlockSpec((1, LANES, V_DIM), lambda s, j, pt: (s, 0, 0))]
                 + [page_blk(i) for i in range(pages)] + [page_blk(i) for i in range(pages)]
                 + [vec, vec, vec, vec, pl.BlockSpec((1, V_DIM), lambda s, j, pt: (0, 0))],
        out_specs=pl.BlockSpec((1, N_HEADS, V_DIM), lambda s, j, pt: (s, 0, 0)),
        scratch_shapes=[pltpu.VMEM((2 * N_HEADS, 1), F32),
                        pltpu.VMEM((2 * N_HEADS, 1), F32),
                        pltpu.VMEM((2 * N_HEADS, V_DIM), F32)],
    )
    return pl.pallas_call(
        functools.partial(_sample_attn_kernel, pages=pages, lam_init=lam_init),
        grid_spec=grid_spec,
        out_shape=jax.ShapeDtypeStruct((nseq, N_HEADS, V_DIM), BF16),
        compiler_params=pltpu.CompilerParams(
            dimension_semantics=("arbitrary", "arbitrary"), vmem_limit_bytes=VMEM_LIMIT),
        name="sample_attn",
    )(page_table, q16, kn_pad, vn_pad, *([cache_k] * pages), *([cache_v] * pages),
      lq1, lk1, lq2, lk2, subln_w)


def _gate_and_group_norm(y, z, nw):
    y = y * _silu(z)
    gw = D_INNER // N_GROUPS
    outs = []
    for g in range(N_GROUPS):
        yg = y[:, g * gw:(g + 1) * gw]
        ms = jnp.mean(yg * yg, axis=-1, keepdims=True)
        outs.append(yg * lax.rsqrt(ms + EPS) * nw[:, g * gw:(g + 1) * gw])
    return outs


def _ssd_prompt_kernel(xbc_ref, z_ref, dt_ref, cw_ref, cb_ref, dtb_ref, alog_ref, dskip_ref, nw_ref,
                       e_ref, tri_ref, ys_ref, ssm_ref, ext_ref, st_ref, y_ref, *, chunk):
    c = pl.program_id(1)
    halo = SUBLANES

    @pl.when(c == 0)
    def _():
        ext_ref[0:halo, :] = jnp.zeros((halo, CONV_DIM), F32)
        st_ref[...] = jnp.zeros(st_ref.shape, F32)

    xt = xbc_ref[0]
    ext_ref[halo:halo + chunk, :] = xt
    acc = cb_ref[...] + ext_ref[halo - 3:halo - 3 + chunk, :] * cw_ref[0:1, :]
    for jj in range(1, CONV_W):
        acc = acc + ext_ref[halo - 3 + jj:halo - 3 + jj + chunk, :] * cw_ref[jj:jj + 1, :]
    u = _silu(acc)
    ext_ref[0:halo, :] = xt[chunk - halo:chunk, :]

    xs = u[:, :D_INNER]
    tri = tri_ref[...]
    e01 = e_ref[...]
    dt = _softplus(dt_ref[0] + dtb_ref[...])
    a = dt * (-jnp.exp(alog_ref[...]))
    a_cum = _exact_left(tri, a)
    a_cum_t = a_cum.T
    a_exp = _exact_right(a_cum, e01)
    dt_exp = _exact_right(dt, e01)
    ea = jnp.exp(a_exp)
    a_last = a_exp[chunk - 1:chunk, :]
    xdt = xs * dt_exp
    xw = (xdt * jnp.exp(a_last - a_exp)).astype(BF16)
    xdt_b = xdt.astype(BF16)
    causal = lax.broadcasted_iota(jnp.int32, (chunk, chunk), 0) >= lax.broadcasted_iota(jnp.int32, (chunk, chunk), 1)
    lane = lax.broadcasted_iota(jnp.int32, (chunk, LANES), 1)
    gw = D_INNER // N_GROUPS

    for g in range(N_GROUPS):
        bg = u[:, D_INNER + g * D_STATE:D_INNER + (g + 1) * D_STATE]
        cg = u[:, D_INNER + (N_GROUPS + g) * D_STATE:D_INNER + (N_GROUPS + g + 1) * D_STATE]
        bg_b, cg_b = bg.astype(BF16), cg.astype(BF16)
        cb = _dot_nt(cg_b, bg_b)
        st_g = st_ref[:, g * gw:(g + 1) * gw]
        y_off = _dot(cg_b, st_g.astype(BF16))
        s_new = _dot(bg.T.astype(BF16), xw[:, g * gw:(g + 1) * gw])
        st_ref[:, g * gw:(g + 1) * gw] = st_g * ea[chunk - 1:chunk, g * gw:(g + 1) * gw] + s_new
        for pp in range(HEADS_PER_GROUP // 2):
            col0 = g * gw + pp * LANES
            xpair = xdt_b[:, col0:col0 + LANES]
            ys_pair = []
            for hh in range(2):
                r = g * HEADS_PER_GROUP + 2 * pp + hh
                seg = a_cum[:, r:r + 1] - a_cum_t[r:r + 1, :]
                decay = jnp.exp(jnp.where(causal, seg, -jnp.inf))
                ys_pair.append(_dot((cb * decay).astype(BF16), xpair))
            y_diag = jnp.where(lane < SSM_HEAD_DIM, ys_pair[0], ys_pair[1])
            y_ref[:, col0:col0 + LANES] = y_diag + ea[:, col0:col0 + LANES] * y_off[:, pp * LANES:(pp + 1) * LANES]

    y = y_ref[...] + dskip_ref[...] * xs
    for g, o in enumerate(_gate_and_group_norm(y, z_ref[0], nw_ref[...])):
        ys_ref[0, :, g * gw:(g + 1) * gw] = o.astype(ys_ref.dtype)

    @pl.when(c == pl.num_programs(1) - 1)
    def _():
        ssm_ref[0] = st_ref[...].T


def _ssd_prompt(xbc, z, dt, conv_w, conv_b, dtb_pad, alog_pad, dskip_exp, nw, e01, tri, chunk):
    b, L, _ = xbc.shape
    full = lambda shape: pl.BlockSpec(shape, lambda bi, c: tuple(0 for _ in shape))
    return pl.pallas_call(
        functools.partial(_ssd_prompt_kernel, chunk=chunk),
        grid=(b, L // chunk),
        in_specs=[pl.BlockSpec((1, chunk, CONV_DIM), lambda bi, c: (bi, c, 0)),
                  pl.BlockSpec((1, chunk, D_INNER), lambda bi, c: (bi, c, 0)),
                  pl.BlockSpec((1, chunk, DT_PAD), lambda bi, c: (bi, c, 0)),
                  full((CONV_W, CONV_DIM)), full((1, CONV_DIM)), full((1, DT_PAD)), full((1, DT_PAD)),
                  full((1, D_INNER)), full((1, D_INNER)), full((LANES, D_INNER)), full((chunk, chunk))],
        out_specs=[pl.BlockSpec((1, chunk, D_INNER), lambda bi, c: (bi, c, 0)),
                   pl.BlockSpec((1, D_INNER, D_STATE), lambda bi, c: (bi, 0, 0))],
        out_shape=[jax.ShapeDtypeStruct((b, L, D_INNER), BF16),
                   jax.ShapeDtypeStruct((b, D_INNER, D_STATE), F32)],
        scratch_shapes=[pltpu.VMEM((SUBLANES + chunk, CONV_DIM), F32),
                        pltpu.VMEM((D_STATE, D_INNER), F32),
                        pltpu.VMEM((chunk, D_INNER), F32)],
        compiler_params=pltpu.CompilerParams(
            dimension_semantics=("arbitrary", "arbitrary"), vmem_limit_bytes=VMEM_LIMIT),
        name="ssd_prompt",
    )(xbc, z, dt, conv_w, conv_b, dtb_pad, alog_pad, dskip_exp, nw, e01, tri)


def _ssd_step_kernel(xbc_ref, z_ref, dt_ref, conv_ref, ssm_ref, cw_ref, cb_ref, dtb_ref, alog_ref, dskip_ref,
                     nw_ref, e_ref, ys_ref, conv_out_ref, ssm_out_ref):
    xn = xbc_ref[0]
    acc = cb_ref[...] + xn * cw_ref[CONV_W - 1:CONV_W, :]
    for jj in range(CONV_W - 1):
        acc = acc + conv_ref[0, jj:jj + 1, :] * cw_ref[jj:jj + 1, :]
        if jj > 0:
            conv_out_ref[0, jj - 1:jj, :] = conv_ref[0, jj:jj + 1, :]
    u = _silu(acc)
    conv_out_ref[0, CONV_W - 2:CONV_W - 1, :] = xn

    xs = u[:, :D_INNER]
    e01 = e_ref[...]
    dt = _softplus(dt_ref[0] + dtb_ref[...])
    d_a = jnp.exp(dt * (-jnp.exp(alog_ref[...])))
    rows8 = lax.broadcasted_iota(jnp.int32, (SUBLANES, DT_PAD), 0)
    two = jnp.where(rows8 == 0, dt, jnp.where(rows8 == 1, d_a, 0.0))
    two_exp = _exact_right(two, e01)
    xdt = xs * two_exp[0:1, :]
    rows = lax.broadcasted_iota(jnp.int32, (LANES, D_INNER), 0)
    stack = jnp.where(rows == 0, xdt, jnp.where(rows == 1, two_exp[1:2, :], 0.0))
    rows_c = lax.broadcasted_iota(jnp.int32, (SUBLANES, D_STATE), 0)
    c8 = jnp.zeros((SUBLANES, D_STATE), F32)
    for g in range(N_GROUPS):
        c_g = u[:, D_INNER + (N_GROUPS + g) * D_STATE:D_INNER + (N_GROUPS + g + 1) * D_STATE]
        c8 = jnp.where(rows_c == g, c_g, c8)
    c8 = c8.astype(BF16)
    blocks_per_group = D_INNER // N_GROUPS // LANES
    y_blocks = []
    for cbk in range(D_INNER // LANES):
        g = cbk // blocks_per_group
        cols = stack[:, cbk * LANES:(cbk + 1) * LANES].T
        b_row = u[:, D_INNER + g * D_STATE:D_INNER + (g + 1) * D_STATE]
        s_old = ssm_ref[0, cbk * LANES:(cbk + 1) * LANES, :]
        s_new = cols[:, 1:2] * s_old + cols[:, 0:1] * b_row
        ssm_out_ref[0, cbk * LANES:(cbk + 1) * LANES, :] = s_new
        yg = _dot_nt(c8, s_new.astype(BF16))
        y_blocks.append(yg[g:g + 1, :])
    y = jnp.concatenate(y_blocks, axis=1) + dskip_ref[...] * xs
    gw = D_INNER // N_GROUPS
    for g, o in enumerate(_gate_and_group_norm(y, z_ref[0], nw_ref[...])):
        ys_ref[0, :, g * gw:(g + 1) * gw] = o.astype(ys_ref.dtype)


def _ssd_step(xbc, z, dt, state_conv, state_ssm, conv_w, conv_b, dtb_pad, alog_pad, dskip_exp, nw, e01):
    nseq = xbc.shape[0]
    full = lambda shape: pl.BlockSpec(shape, lambda s: tuple(0 for _ in shape))
    per = lambda shape: pl.BlockSpec((1,) + shape, lambda s: (s, 0, 0))
    return pl.pallas_call(
        _ssd_step_kernel,
        grid=(nseq,),
        in_specs=[per((1, CONV_DIM)), per((1, D_INNER)), per((1, DT_PAD)), per((CONV_W - 1, CONV_DIM)),
                  per((D_INNER, D_STATE)),
                  full((CONV_W, CONV_DIM)), full((1, CONV_DIM)), full((1, DT_PAD)), full((1, DT_PAD)),
                  full((1, D_INNER)), full((1, D_INNER)), full((LANES, D_INNER))],
        out_specs=[per((1, D_INNER)), per((CONV_W - 1, CONV_DIM)), per((D_INNER, D_STATE))],
        out_shape=[jax.ShapeDtypeStruct((nseq, 1, D_INNER), BF16),
                   jax.ShapeDtypeStruct((nseq, CONV_W - 1, CONV_DIM), F32),
                   jax.ShapeDtypeStruct((nseq, D_INNER, D_STATE), F32)],
        compiler_params=pltpu.CompilerParams(dimension_semantics=("arbitrary",), vmem_limit_bytes=VMEM_LIMIT),
        name="ssd_step",
    )(xbc, z, dt, state_conv, state_ssm, conv_w, conv_b, dtb_pad, alog_pad, dskip_exp, nw, e01)


def _merge_kernel(x_ref, o_ref, ys_ref, g_ref, g1_ref, wa_ref, wb_ref, wo_ref, x1_ref):
    ya = _dot(o_ref[...], wa_ref[...])
    yb = _dot(ys_ref[...], wb_ref[...])
    gates = _sigmoid(g_ref[...])
    mix = gates[:, :D_MODEL] * ya + gates[:, D_MODEL:] * yb
    mixed = _dot(mix.astype(BF16), wo_ref[...])
    x1_ref[...] = x_ref[...] + g1_ref[0] * mixed


def _merge(x2d, o2d, ys2d, gate2d, g1, wa, wb, wo, tm, rows_per_mod):
    rows = x2d.shape[0]
    mod_rows = g1.shape[1]
    tiles_per_mod = rows_per_mod // tm
    row_blk = lambda width: pl.BlockSpec((tm, width), lambda i: (i, 0))
    wfull = lambda shape: pl.BlockSpec(shape, lambda i: (0, 0), pipeline_mode=pl.Buffered(1))
    return pl.pallas_call(
        _merge_kernel,
        grid=(rows // tm,),
        in_specs=[row_blk(D_MODEL), row_blk(ATTN_WIDTH), row_blk(D_INNER), row_blk(GATE_WIDTH),
                  pl.BlockSpec((1, mod_rows, D_MODEL), lambda i: (i // tiles_per_mod, 0, 0)),
                  wfull((ATTN_WIDTH, D_MODEL)), wfull((D_INNER, D_MODEL)), wfull((D_MODEL, D_MODEL))],
        out_specs=row_blk(D_MODEL),
        out_shape=jax.ShapeDtypeStruct((rows, D_MODEL), F32),
        compiler_params=pltpu.CompilerParams(dimension_semantics=("arbitrary",), vmem_limit_bytes=VMEM_LIMIT),
        name="merge",
    )(x2d, o2d, ys2d, gate2d, g1, wa, wb, wo)


def _ffn_kernel(x_ref, sh_ref, sc_ref, g2_ref, nw_ref, fw_ref, wg_ref, wu_ref, wd_ref, y_ref):
    x = x_ref[...]
    ms = jnp.mean(x * x, axis=-1, keepdims=True)
    h = x * lax.rsqrt(ms + EPS) * nw_ref[...]
    hb = (h * (1.0 + sc_ref[0]) + sh_ref[0]).astype(BF16)
    ff = _silu(_dot(hb, wg_ref[...])) * _dot(hb, wu_ref[...])
    x2 = x + g2_ref[0] * _dot(ff.astype(BF16), wd_ref[...])
    ms2 = jnp.mean(x2 * x2, axis=-1, keepdims=True)
    y_ref[...] = x2 * lax.rsqrt(ms2 + EPS) * fw_ref[...]


def _ffn(x2d, sh, sc, g2, nw, fw, wg, wu, wd, tm, rows_per_mod):
    rows = x2d.shape[0]
    mod_rows = sh.shape[1]
    tiles_per_mod = rows_per_mod // tm
    row_blk = pl.BlockSpec((tm, D_MODEL), lambda i: (i, 0))
    mod_blk = pl.BlockSpec((1, mod_rows, D_MODEL), lambda i: (i // tiles_per_mod, 0, 0))
    vec = pl.BlockSpec((1, D_MODEL), lambda i: (0, 0))
    wfull = lambda shape: pl.BlockSpec(shape, lambda i: (0, 0), pipeline_mode=pl.Buffered(1))
    return pl.pallas_call(
        _ffn_kernel,
        grid=(rows // tm,),
        in_specs=[row_blk, mod_blk, mod_blk, mod_blk, vec, vec,
                  wfull((D_MODEL, D_FF)), wfull((D_MODEL, D_FF)), wfull((D_FF, D_MODEL))],
        out_specs=row_blk,
        out_shape=jax.ShapeDtypeStruct((rows, D_MODEL), F32),
        compiler_params=pltpu.CompilerParams(dimension_semantics=("arbitrary",), vmem_limit_bytes=VMEM_LIMIT),
        name="ffn",
    )(x2d, sh, sc, g2, nw, fw, wg, wu, wd)


def _layer(x2d, mods, pos, attend, mamba, p, tm_proj, tm_out, rows_per_mod):
    sh1, sc1, g1, sh2, sc2, g2 = mods
    ct, s1, s2 = _rope_tables(pos)
    qb, k, kb, v, vb, z, xbc, gate, dt = _in_proj(x2d, sh1, sc1, p['norm1_w'], p['w_in'], ct, s1, s2,
                                                  tm_proj, rows_per_mod)
    o = attend(qb, k, kb, v, vb)
    ys, new_conv, new_ssm = mamba(z, xbc, dt)
    x1 = _merge(x2d, o, ys, gate, g1, p['w_branch_attn'], p['w_branch_ssm'], p['w_out'], tm_out, rows_per_mod)
    y = _ffn(x1, sh2, sc2, g2, p['norm2_w'], p['final_norm_w'], p['w_ffn_gate'], p['w_ffn_up'], p['w_ffn_down'],
             tm_out, rows_per_mod)
    return y, k, v, new_conv, new_ssm


def kernel(x_prompt, x_sample, c_prompt, c_sample, cache_k, cache_v, page_table, state_conv, state_ssm, w_ada, b_ada, norm1_w, w_in, lambda_q1, lambda_k1, lambda_q2, lambda_k2, subln_w, conv_w, conv_b, dt_bias, a_log, d_skip, ssm_norm_w, w_branch_attn, w_branch_ssm, w_out, norm2_w, w_ffn_gate, w_ffn_up, w_ffn_down, final_norm_w):
    depth = w_in.shape[0]
    assert depth == 1
    bp, L, d = x_prompt.shape
    ns, Ld, _ = x_sample.shape
    assert Ld == 1 and d == D_MODEL
    n_pages = page_table.shape[1]
    past = n_pages * PAGE_SIZE
    lam_init = 0.8 - 0.6 * math.exp(-0.3 * 0)
    i = 0

    splits = np.cumsum([QK_WIDTH, QK_WIDTH, ATTN_WIDTH, D_INNER, CONV_DIM, SSM_HEADS])
    wq, wk, wv, wz, wx, wdt, wg = jnp.split(w_in[i], splits, axis=1)
    w_in_r = jnp.concatenate(
        [wq, wk, wv, wz, wx, wg, wdt, jnp.zeros((D_MODEL, DT_PAD - SSM_HEADS), F32)], axis=1).astype(BF16)
    pad_heads = lambda t: jnp.concatenate([t, jnp.zeros((DT_PAD - SSM_HEADS,), F32)]).reshape(1, DT_PAD)
    e01 = (jnp.arange(LANES)[:, None] == (jnp.arange(D_INNER)[None, :] // SSM_HEAD_DIM)).astype(BF16)
    p = {
        'norm1_w': norm1_w[i].reshape(1, -1), 'w_in': w_in_r,
        'w_branch_attn': w_branch_attn[i].astype(BF16), 'w_branch_ssm': w_branch_ssm[i].astype(BF16),
        'w_out': w_out[i].astype(BF16), 'norm2_w': norm2_w[i].reshape(1, -1),
        'final_norm_w': final_norm_w.reshape(1, -1),
        'w_ffn_gate': w_ffn_gate[i].astype(BF16), 'w_ffn_up': w_ffn_up[i].astype(BF16),
        'w_ffn_down': w_ffn_down[i].astype(BF16),
    }
    lam_vecs = [t[i].reshape(1, HEAD_DIM) for t in (lambda_q1, lambda_k1, lambda_q2, lambda_k2)]
    sw = subln_w[i].reshape(1, V_DIM)
    mamba_params = (conv_w[i], conv_b[i].reshape(1, -1), pad_heads(dt_bias[i]), pad_heads(a_log[i]),
                    jnp.repeat(d_skip[i], SSM_HEAD_DIM).reshape(1, D_INNER), ssm_norm_w[i].reshape(1, -1), e01)

    n_c = bp + ns
    c_rows = -(-n_c // SUBLANES) * SUBLANES
    c_all = jnp.concatenate([c_prompt, c_sample, jnp.zeros((c_rows - n_c, d), F32)], axis=0)
    mod = _ada_mod(c_all, w_ada[i], b_ada[i])
    mods_p = [m.reshape(bp, 1, d) for m in jnp.split(mod[:bp], 6, axis=1)]
    mods_s = [m.reshape(1, ns, d) for m in jnp.split(mod[bp:bp + ns], 6, axis=1)]

    chunk = 128
    tri = jnp.tril(jnp.ones((chunk, chunk), F32)).astype(BF16)

    def attend_p(qb, k, kb, v, vb):
        r3 = lambda t: t.reshape(bp, L, -1)
        o = _prompt_attention(r3(qb), r3(kb), r3(vb), *lam_vecs, sw, lam_init, tq=512)
        return o.reshape(bp * L, ATTN_WIDTH)

    def mamba_p(z, xbc, dt):
        xbc3 = xbc.reshape(bp, L, CONV_DIM)
        ys, ssm = _ssd_prompt(xbc3, z.reshape(bp, L, D_INNER), dt.reshape(bp, L, DT_PAD),
                              *mamba_params[:6], e01, tri, chunk)
        return ys.reshape(bp * L, D_INNER), xbc3[:, L - (CONV_W - 1):, :], ssm

    yp, kp, vp, cp, sp = _layer(x_prompt.reshape(bp * L, d), mods_p, jnp.arange(L), attend_p, mamba_p, p,
                                tm_proj=256, tm_out=512, rows_per_mod=L)

    pool = cache_k.shape[1]
    ck = cache_k[i].reshape(pool, PAGE_SIZE * N_HEADS, V_DIM)
    cv = cache_v[i].reshape(pool, PAGE_SIZE * N_HEADS, V_DIM)

    def attend_s(qb, k, kb, v, vb):
        qh = qb.reshape(ns, N_HEADS, V_DIM)
        lane = jnp.arange(V_DIM)[None, None, :]
        q16 = jnp.concatenate([jnp.where(lane < HEAD_DIM, qh, 0), jnp.where(lane >= HEAD_DIM, qh, 0)], axis=1)
        padn = lambda t: jnp.concatenate(
            [t.reshape(ns, N_HEADS, V_DIM), jnp.zeros((ns, LANES - N_HEADS, V_DIM), F32)], axis=1)
        o = _sample_attention(q16, padn(k), padn(v), ck, cv, page_table, *lam_vecs, sw, lam_init, pages=8)
        return o.reshape(ns, ATTN_WIDTH)

    def mamba_s(z, xbc, dt):
        ys, conv, ssm = _ssd_step(xbc.reshape(ns, 1, CONV_DIM), z.reshape(ns, 1, D_INNER), dt.reshape(ns, 1, DT_PAD),
                                  state_conv[i], state_ssm[i].reshape(ns, D_INNER, D_STATE), *mamba_params)
        return ys.reshape(ns, D_INNER), conv, ssm

    ys_, ks, vs, cs, ss = _layer(x_sample.reshape(ns, d), mods_s, jnp.full((ns,), past), attend_s, mamba_s, p,
                                 tm_proj=ns, tm_out=ns, rows_per_mod=ns)

    hk = (N_HEADS, 2 * HEAD_DIM)
    return (yp.reshape(bp, L, d), ys_.reshape(ns, 1, d),
            kp.reshape((1, bp, L) + hk), vp.reshape((1, bp, L) + hk),
            cp.reshape(1, bp, CONV_W - 1, CONV_DIM), sp.reshape(1, bp, SSM_HEADS, SSM_HEAD_DIM, D_STATE),
            ks.reshape((1, ns, 1) + hk), vs.reshape((1, ns, 1) + hk),
            cs.reshape(1, ns, CONV_W - 1, CONV_DIM), ss.reshape(1, ns, SSM_HEADS, SSM_HEAD_DIM, D_STATE))
```

```python
import functools
import math

import jax
import jax.numpy as jnp
import numpy as np
from jax import lax
from jax.experimental import pallas as pl
from jax.experimental.pallas import tpu as pltpu

F32 = jnp.float32
BF16 = jnp.bfloat16

D_MODEL = 1024
N_HEADS = 8
HEAD_DIM = 64
V_DIM = 2 * HEAD_DIM
QK_WIDTH = N_HEADS * 2 * HEAD_DIM
ATTN_WIDTH = N_HEADS * V_DIM
ROPE_DIM = HEAD_DIM // 4
ROPE_THETA = 500000.0
D_INNER = 2 * D_MODEL
SSM_HEAD_DIM = 64
SSM_HEADS = D_INNER // SSM_HEAD_DIM
N_GROUPS = 4
HEADS_PER_GROUP = SSM_HEADS // N_GROUPS
D_STATE = 128
CONV_W = 4
CONV_DIM = D_INNER + 2 * N_GROUPS * D_STATE
D_FF = -(-8 * D_MODEL // (3 * 256)) * 256
GATE_WIDTH = 2 * D_MODEL
PAGE_SIZE = 128
EPS = 1e-6

LANES = 128
SUBLANES = 8
DT_PAD = LANES
IN_WIDTH_PAD = 2 * QK_WIDTH + ATTN_WIDTH + D_INNER + CONV_DIM + GATE_WIDTH + DT_PAD
VMEM_LIMIT = 56 * 1024 * 1024

_OQ = 0
_OK = _OQ + QK_WIDTH
_OV = _OK + QK_WIDTH
_OZ = _OV + ATTN_WIDTH
_OX = _OZ + D_INNER
_OG = _OX + CONV_DIM
_OD = _OG + GATE_WIDTH


def _silu(x):
    return x * (1.0 / (1.0 + jnp.exp(-x)))


def _sigmoid(x):
    return 1.0 / (1.0 + jnp.exp(-x))


def _softplus(x):
    return jnp.maximum(x, 0.0) + jnp.log(1.0 + jnp.exp(-jnp.abs(x)))


def _split3(a):
    hi = a.astype(BF16)
    r1 = a - hi.astype(F32)
    mid = r1.astype(BF16)
    lo = (r1 - mid.astype(F32)).astype(BF16)
    return hi, mid, lo


def _dot(a, b):
    return jnp.dot(a, b, preferred_element_type=F32)


def _dot_nt(a, b):
    return lax.dot_general(a, b, (((1,), (1,)), ((), ())), preferred_element_type=F32)


def _exact_right(a, e01):
    hi, mid, lo = _split3(a)
    return _dot(hi, e01) + _dot(mid, e01) + _dot(lo, e01)


def _exact_left(t01, a):
    hi, mid, lo = _split3(a)
    return _dot(t01, hi) + _dot(t01, mid) + _dot(t01, lo)


def _ada_kernel(c_ref, w_ref, b_ref, o_ref):
    c = _silu(c_ref[...]).astype(BF16)
    o_ref[...] = _dot(c, w_ref[...].astype(BF16)) + b_ref[...]


def _ada_mod(c_all, w_ada, b_ada):
    rows = c_all.shape[0]
    tn = D_MODEL
    return pl.pallas_call(
        _ada_kernel,
        grid=(6 * D_MODEL // tn,),
        in_specs=[pl.BlockSpec((rows, D_MODEL), lambda j: (0, 0)),
                  pl.BlockSpec((D_MODEL, tn), lambda j: (0, j)),
                  pl.BlockSpec((1, tn), lambda j: (0, j))],
        out_specs=pl.BlockSpec((rows, tn), lambda j: (0, j)),
        out_shape=jax.ShapeDtypeStruct((rows, 6 * D_MODEL), F32),
        name="ada_mod",
    )(c_all, w_ada, b_ada.reshape(1, -1))


def _rope_cols(t, c_tab, s1_tab, s2_tab, scale):
    outs = []
    for cb in range(t.shape[1] // LANES):
        tc = t[:, cb * LANES:(cb + 1) * LANES]
        up = pltpu.roll(tc, LANES - ROPE_DIM // 2, axis=1)
        dn = pltpu.roll(tc, ROPE_DIM // 2, axis=1)
        r = tc * c_tab + up * s1_tab + dn * s2_tab
        outs.append(r * scale if scale != 1.0 else r)
    return outs


def _inproj_kernel(x_ref, sh_ref, sc_ref, nw_ref, w_ref, ct_ref, s1_ref, s2_ref,
                   q_ref, k_ref, kb_ref, v_ref, z_ref, xbc_ref, g_ref, dt_ref, *maybe_vt_ref):
    x = x_ref[...]
    ms = jnp.mean(x * x, axis=-1, keepdims=True)
    h = x * lax.rsqrt(ms + EPS) * nw_ref[...]
    h = h * (1.0 + sc_ref[0]) + sh_ref[0]
    hb = h.astype(BF16)

    def proj(lo, width):
        return _dot(hb, w_ref[:, lo:lo + width])

    ct, s1, s2 = ct_ref[...], s1_ref[...], s2_ref[...]
    q = _rope_cols(proj(_OQ, QK_WIDTH), ct, s1, s2, math.log2(math.e) / math.sqrt(HEAD_DIM))
    for cb, r in enumerate(q):
        q_ref[:, cb * LANES:(cb + 1) * LANES] = r.astype(BF16)
    k = _rope_cols(proj(_OK, QK_WIDTH), ct, s1, s2, 1.0)
    for cb, r in enumerate(k):
        k_ref[:, cb * LANES:(cb + 1) * LANES] = r
        kb_ref[:, cb * LANES:(cb + 1) * LANES] = r.astype(BF16)
    v = proj(_OV, ATTN_WIDTH)
    v_ref[...] = v
    for vt_ref in maybe_vt_ref:
        for cb in range(ATTN_WIDTH // LANES):
            vt_ref[0, cb * LANES:(cb + 1) * LANES, :] = v[:, cb * LANES:(cb + 1) * LANES].T.astype(BF16)
    z_ref[...] = proj(_OZ, D_INNER)
    xbc_ref[...] = proj(_OX, CONV_DIM)
    g_ref[...] = proj(_OG, GATE_WIDTH)
    dt_ref[...] = proj(_OD, DT_PAD)


def _in_proj(x2d, sh, sc, nw, w_in, ctab, s1tab, s2tab, tm, rows_per_mod, with_vt):
    rows = x2d.shape[0]
    mod_rows = sh.shape[1]
    tiles_per_mod = rows_per_mod // tm
    row_blk = lambda width: pl.BlockSpec((tm, width), lambda i: (i, 0))
    mod_blk = pl.BlockSpec((1, mod_rows, D_MODEL), lambda i: (i // tiles_per_mod, 0, 0))
    assert ctab.shape[0] == rows_per_mod
    tab_blk = pl.BlockSpec((tm, LANES), lambda i: (i % tiles_per_mod, 0))
    outs = [(QK_WIDTH, BF16), (QK_WIDTH, F32), (QK_WIDTH, BF16), (ATTN_WIDTH, F32),
            (D_INNER, F32), (CONV_DIM, F32), (GATE_WIDTH, F32), (DT_PAD, F32)]
    out_specs = [row_blk(w) for w, _ in outs]
    out_shape = [jax.ShapeDtypeStruct((rows, w), dt) for w, dt in outs]
    if with_vt:
        out_specs.append(pl.BlockSpec((1, ATTN_WIDTH, tm), lambda i: (i, 0, 0)))
        out_shape.append(jax.ShapeDtypeStruct((rows // tm, ATTN_WIDTH, tm), BF16))
    return pl.pallas_call(
        _inproj_kernel,
        grid=(rows // tm,),
        in_specs=[row_blk(D_MODEL), mod_blk, mod_blk,
                  pl.BlockSpec((1, D_MODEL), lambda i: (0, 0)),
                  pl.BlockSpec((D_MODEL, IN_WIDTH_PAD), lambda i: (0, 0), pipeline_mode=pl.Buffered(1)),
                  tab_blk, tab_blk, tab_blk],
        out_specs=out_specs,
        out_shape=out_shape,
        compiler_params=pltpu.CompilerParams(dimension_semantics=("arbitrary",), vmem_limit_bytes=VMEM_LIMIT),
        name="in_proj",
    )(x2d, sh, sc, nw, w_in, ctab, s1tab, s2tab)


def _rope_tables(pos):
    half = ROPE_DIM // 2
    inv_freq = ROPE_THETA ** (-jnp.arange(half, dtype=F32) * 2.0 / ROPE_DIM)
    ang = pos.astype(F32)[:, None] * inv_freq[None, :]
    cos, sin = jnp.cos(ang), jnp.sin(ang)
    n = pos.shape[0]
    rest = HEAD_DIM - ROPE_DIM
    c64 = jnp.concatenate([cos, cos, jnp.ones((n, rest), F32)], axis=1)
    s1_64 = jnp.concatenate([-sin, jnp.zeros((n, half + rest), F32)], axis=1)
    s2_64 = jnp.concatenate([jnp.zeros((n, half), F32), sin, jnp.zeros((n, rest), F32)], axis=1)
    rep = LANES // HEAD_DIM
    return jnp.tile(c64, (1, rep)), jnp.tile(s1_64, (1, rep)), jnp.tile(s2_64, (1, rep))


def _lambda_value(lq1, lk1, lq2, lk2, lam_init):
    a = jnp.sum(lq1[...] * lk1[...], axis=1, keepdims=True)
    b = jnp.sum(lq2[...] * lk2[...], axis=1, keepdims=True)
    return jnp.exp(a) - jnp.exp(b) + lam_init


def _prompt_attn_kernel(q_ref, k_ref, vt_ref, lq1, lk1, lq2, lk2, sw_ref, o_ref,
                        qm_ref, sa_ref, sb_ref, m_ref, acc_ref, *, tq, tk, lam_init):
    assert tq == 2 * tk
    qi = pl.program_id(2)
    q = q_ref[0]
    lane = lax.broadcasted_iota(jnp.int32, q.shape, 1)
    zero = jnp.zeros_like(q)
    qm_ref[0] = jnp.where(lane < HEAD_DIM, q, zero)
    qm_ref[1] = jnp.where(lane >= HEAD_DIM, q, zero)
    m_ref[...] = jnp.full(m_ref.shape, -jnp.inf, F32)
    acc_ref[...] = jnp.zeros(acc_ref.shape, F32)

    def scores(ki, s_ref):
        k = k_ref[0, pl.ds(pl.multiple_of(ki * tk, tk), tk), :]
        for mp in range(2):
            s_ref[mp] = _dot_nt(k, qm_ref[mp])

    def absorb(ki, s_ref, diag):
        vt = jnp.concatenate([vt_ref[ki], jnp.ones((acc_ref.shape[1] - V_DIM, tk), BF16)], axis=0)
        for mp in range(2):
            st = s_ref[mp]
            if diag is not None:
                key = diag * tk + lax.broadcasted_iota(jnp.int32, st.shape, 0)
                qry = lax.broadcasted_iota(jnp.int32, st.shape, 1)
                st = jnp.where(key <= qry, st, -jnp.inf)
            m_prev = m_ref[mp]
            m_new = jnp.maximum(m_prev, jnp.max(st, axis=0, keepdims=True))
            alpha = jnp.exp2(m_prev - m_new)
            pt = jnp.exp2(st - m_new)
            acc_ref[mp] = alpha * acc_ref[mp] + _dot(vt, pt.astype(BF16))
            m_ref[mp] = m_new

    n_full = 2 * qi
    scores(0, sa_ref)

    def body(j, carry):
        scores(2 * j + 1, sb_ref)
        absorb(2 * j, sa_ref, None)
        scores(2 * j + 2, sa_ref)
        absorb(2 * j + 1, sb_ref, None)
        return carry

    lax.fori_loop(0, qi, body, 0)
    scores(n_full + 1, sb_ref)
    absorb(n_full, sa_ref, 0)
    absorb(n_full + 1, sb_ref, 1)

    lam = _lambda_value(lq1, lk1, lq2, lk2, lam_init)
    o1 = acc_ref[0, 0:V_DIM, :] / acc_ref[0, V_DIM:V_DIM + 1, :]
    o2 = acc_ref[1, 0:V_DIM, :] / acc_ref[1, V_DIM:V_DIM + 1, :]
    ot = o1 - lam * o2
    ms = jnp.mean(ot * ot, axis=0, keepdims=True)
    ot = ot * lax.rsqrt(ms + EPS) * sw_ref[...] * (1.0 - lam_init)
    o_ref[0] = ot.T.astype(o_ref.dtype)


def _prompt_attention(qb, kb, vt, lq1, lk1, lq2, lk2, subln_col, lam_init, tq):
    b, L, _ = qb.shape
    tk = vt.shape[2]
    vec = pl.BlockSpec((1, HEAD_DIM), lambda bi, h, qi: (0, 0))
    q_blk = pl.BlockSpec((1, tq, V_DIM), lambda bi, h, qi: (bi, qi, h))
    return pl.pallas_call(
        functools.partial(_prompt_attn_kernel, tq=tq, tk=tk, lam_init=lam_init),
        grid=(b, N_HEADS, L // tq),
        in_specs=[q_blk,
                  pl.BlockSpec((1, L, V_DIM), lambda bi, h, qi: (bi, 0, h)),
                  pl.BlockSpec((L // tk, V_DIM, tk), lambda bi, h, qi: (bi, h, 0)),
                  vec, vec, vec, vec,
                  pl.BlockSpec((V_DIM, 1), lambda bi, h, qi: (0, 0))],
        out_specs=q_blk,
        out_shape=jax.ShapeDtypeStruct((b, L, ATTN_WIDTH), BF16),
        scratch_shapes=[pltpu.VMEM((2, tq, V_DIM), BF16),
                        pltpu.VMEM((2, tk, tq), F32),
                        pltpu.VMEM((2, tk, tq), F32),
                        pltpu.VMEM((2, 1, tq), F32),
                        pltpu.VMEM((2, V_DIM + 16, tq), F32)],
        compiler_params=pltpu.CompilerParams(
            dimension_semantics=("arbitrary", "arbitrary", "arbitrary"), vmem_limit_bytes=VMEM_LIMIT),
        name="prompt_attn",
    )(qb, kb, vt, lq1, lk1, lq2, lk2, subln_col)


def _sample_attn_kernel(pt_ref, q_ref, kn_ref, vn_ref, *rest, pages, group, lam_init):
    k_refs = rest[:pages]
    v_refs = rest[pages:2 * pages]
    lq1, lk1, lq2, lk2, sw_ref, o_ref, m_ref, l_ref, acc_ref = rest[2 * pages:]
    j = pl.program_id(1)
    nrow = 2 * N_HEADS
    q16 = q_ref[0]

    def head_mask(nkeys):
        row = lax.broadcasted_iota(jnp.int32, (nrow, nkeys), 0)
        col = lax.broadcasted_iota(jnp.int32, (nrow, nkeys), 1)
        return (row % N_HEADS) == (col % N_HEADS)

    def scores(ks, valid):
        return [jnp.where(valid, _dot(q16, k2d.astype(BF16).T), -jnp.inf) for k2d in ks]

    def absorb(ss, vs):
        m_blk = jnp.max(ss[0], axis=1, keepdims=True)
        for s in ss[1:]:
            m_blk = jnp.maximum(m_blk, jnp.max(s, axis=1, keepdims=True))
        m_prev = m_ref[...]
        m_new = jnp.maximum(m_prev, m_blk)
        alpha = jnp.exp2(m_prev - m_new)
        l_new = alpha * l_ref[...]
        acc = alpha * acc_ref[...]
        for s, v2d in zip(ss, vs):
            p = jnp.exp2(s - m_new)
            l_new = l_new + jnp.sum(p, axis=1, keepdims=True)
            acc = acc + _dot(p.astype(BF16), v2d)
        l_ref[...] = l_new
        acc_ref[...] = acc
        m_ref[...] = m_new

    @pl.when(j == 0)
    def _():
        m_ref[...] = jnp.full(m_ref.shape, -jnp.inf, F32)
        l_ref[...] = jnp.zeros(l_ref.shape, F32)
        acc_ref[...] = jnp.zeros(acc_ref.shape, F32)
        col = lax.broadcasted_iota(jnp.int32, (nrow, LANES), 1)
        absorb(scores([kn_ref[0]], head_mask(LANES) & (col < N_HEADS)), [vn_ref[0].astype(BF16)])

    valid = head_mask(PAGE_SIZE * N_HEADS)
    bounds = list(range(0, pages + 1, group))
    ss = scores([r[0] for r in k_refs[0:group]], valid)
    for lo, hi in zip(bounds[:-1], bounds[1:]):
        nxt = scores([r[0] for r in k_refs[hi:hi + group]], valid) if hi < pages else None
        absorb(ss, [r[0].astype(BF16) for r in v_refs[lo:hi]])
        ss = nxt

    @pl.when(j == pl.num_programs(1) - 1)
    def _():
        lam = _lambda_value(lq1, lk1, lq2, lk2, lam_init)
        o1 = acc_ref[0:N_HEADS, :] / l_ref[0:N_HEADS, :]
        o2 = acc_ref[N_HEADS:nrow, :] / l_ref[N_HEADS:nrow, :]
        o = o1 - lam * o2
        ms = jnp.mean(o * o, axis=-1, keepdims=True)
        o_ref[0] = (o * lax.rsqrt(ms + EPS) * sw_ref[...] * (1.0 - lam_init)).astype(o_ref.dtype)


def _sample_attention(q16, kn_pad, vn_pad, cache_k, cache_v, page_table, lq1, lk1, lq2, lk2, subln_w,
                      lam_init, pages, group):
    nseq, n_pages = page_table.shape
    assert n_pages % pages == 0 and pages % group == 0
    steps = n_pages // pages

    def page_blk(i):
        return pl.BlockSpec((1, PAGE_SIZE * N_HEADS, V_DIM), lambda s, j, pt: (pt[s, j * pages + i], 0, 0))

    vec = pl.BlockSpec((1, HEAD_DIM), lambda s, j, pt: (0, 0))
    grid_spec = pltpu.PrefetchScalarGridSpec(
        num_scalar_prefetch=1,
        grid=(nseq, steps),
        in_specs=[pl.BlockSpec((1, 2 * N_HEADS, V_DIM), lambda s, j, pt: (s, 0, 0)),
                  pl.BlockSpec((1, LANES, V_DIM), lambda s, j, pt: (s, 0, 0)),
                  pl.BlockSpec((1, LANES, V_DIM), lambda s, j, pt: (s, 0, 0))]
                 + [page_blk(i) for i in range(pages)] + [page_blk(i) for i in range(pages)]
                 + [vec, vec, vec, vec, pl.BlockSpec((1, V_DIM), lambda s, j, pt: (0, 0))],
        out_specs=pl.BlockSpec((1, N_HEADS, V_DIM), lambda s, j, pt: (s, 0, 0)),
        scratch_shapes=[pltpu.VMEM((2 * N_HEADS, 1), F32),
                        pltpu.VMEM((2 * N_HEADS, 1), F32),
                        pltpu.VMEM((2 * N_HEADS, V_DIM), F32)],
    )
    return pl.pallas_call(
        functools.partial(_sample_attn_kernel, pages=pages, group=group, lam_init=lam_init),
        grid_spec=grid_spec,
        out_shape=jax.ShapeDtypeStruct((nseq, N_HEADS, V_DIM), BF16),
        compiler_params=pltpu.CompilerParams(
            dimension_semantics=("arbitrary", "arbitrary"), vmem_limit_bytes=VMEM_LIMIT),
        name="sample_attn",
    )(page_table, q16, kn_pad, vn_pad, *([cache_k] * pages), *([cache_v] * pages),
      lq1, lk1, lq2, lk2, subln_w)


def _gate_and_group_norm(y, z, nw):
    y = y * _silu(z)
    gw = D_INNER // N_GROUPS
    outs = []
    for g in range(N_GROUPS):
        yg = y[:, g * gw:(g + 1) * gw]
        ms = jnp.mean(yg * yg, axis=-1, keepdims=True)
        outs.append(yg * lax.rsqrt(ms + EPS) * nw[:, g * gw:(g + 1) * gw])
    return outs


def _ssd_prompt_kernel(xbc_ref, z_ref, dt_ref, cw_ref, cb_ref, dtb_ref, alog_ref, dskip_ref, nw_ref,
                       e_ref, tri_ref, ys_ref, ssm_ref, ext_ref, st_ref, y_ref, *, chunk):
    c = pl.program_id(1)
    halo = SUBLANES

    @pl.when(c == 0)
    def _():
        ext_ref[0:halo, :] = jnp.zeros((halo, CONV_DIM), F32)
        st_ref[...] = jnp.zeros(st_ref.shape, F32)

    xt = xbc_ref[0]
    ext_ref[halo:halo + chunk, :] = xt
    acc = cb_ref[...] + ext_ref[halo - 3:halo - 3 + chunk, :] * cw_ref[0:1, :]
    for jj in range(1, CONV_W):
        acc = acc + ext_ref[halo - 3 + jj:halo - 3 + jj + chunk, :] * cw_ref[jj:jj + 1, :]
    u = _silu(acc)
    ext_ref[0:halo, :] = xt[chunk - halo:chunk, :]

    xs = u[:, :D_INNER]
    tri = tri_ref[...]
    e01 = e_ref[...]
    dt = _softplus(dt_ref[0] + dtb_ref[...])
    a = dt * (-jnp.exp(alog_ref[...]))
    a_cum = _exact_left(tri, a)
    a_cum_t = a_cum.T
    a_exp = _exact_right(a_cum, e01)
    dt_exp = _exact_right(dt, e01)
    ea = jnp.exp(a_exp)
    a_last = a_exp[chunk - 1:chunk, :]
    xdt = xs * dt_exp
    xw = (xdt * jnp.exp(a_last - a_exp)).astype(BF16)
    xdt_b = xdt.astype(BF16)
    causal = lax.broadcasted_iota(jnp.int32, (chunk, chunk), 0) >= lax.broadcasted_iota(jnp.int32, (chunk, chunk), 1)
    lane = lax.broadcasted_iota(jnp.int32, (chunk, LANES), 1)
    gw = D_INNER // N_GROUPS

    for g in range(N_GROUPS):
        bg = u[:, D_INNER + g * D_STATE:D_INNER + (g + 1) * D_STATE]
        cg = u[:, D_INNER + (N_GROUPS + g) * D_STATE:D_INNER + (N_GROUPS + g + 1) * D_STATE]
        bg_b, cg_b = bg.astype(BF16), cg.astype(BF16)
        cb = _dot_nt(cg_b, bg_b)
        st_g = st_ref[:, g * gw:(g + 1) * gw]
        y_off = _dot(cg_b, st_g.astype(BF16))
        s_new = _dot(bg.T.astype(BF16), xw[:, g * gw:(g + 1) * gw])
        st_ref[:, g * gw:(g + 1) * gw] = st_g * ea[chunk - 1:chunk, g * gw:(g + 1) * gw] + s_new
        for pp in range(HEADS_PER_GROUP // 2):
            col0 = g * gw + pp * LANES
            xpair = xdt_b[:, col0:col0 + LANES]
            ys_pair = []
            for hh in range(2):
                r = g * HEADS_PER_GROUP + 2 * pp + hh
                seg = a_cum[:, r:r + 1] - a_cum_t[r:r + 1, :]
                decay = jnp.exp(jnp.where(causal, seg, -jnp.inf))
                ys_pair.append(_dot((cb * decay).astype(BF16), xpair))
            y_diag = jnp.where(lane < SSM_HEAD_DIM, ys_pair[0], ys_pair[1])
            y_ref[:, col0:col0 + LANES] = y_diag + ea[:, col0:col0 + LANES] * y_off[:, pp * LANES:(pp + 1) * LANES]

    y = y_ref[...] + dskip_ref[...] * xs
    for g, o in enumerate(_gate_and_group_norm(y, z_ref[0], nw_ref[...])):
        ys_ref[0, :, g * gw:(g + 1) * gw] = o.astype(ys_ref.dtype)

    @pl.when(c == pl.num_programs(1) - 1)
    def _():
        ssm_ref[0] = st_ref[...].T


def _ssd_prompt(xbc, z, dt, conv_w, conv_b, dtb_pad, alog_pad, dskip_exp, nw, e01, tri, chunk):
    b, L, _ = xbc.shape
    full = lambda shape: pl.BlockSpec(shape, lambda bi, c: tuple(0 for _ in shape))
    return pl.pallas_call(
        functools.partial(_ssd_prompt_kernel, chunk=chunk),
        grid=(b, L // chunk),
        in_specs=[pl.BlockSpec((1, chunk, CONV_DIM), lambda bi, c: (bi, c, 0)),
                  pl.BlockSpec((1, chunk, D_INNER), lambda bi, c: (bi, c, 0)),
                  pl.BlockSpec((1, chunk, DT_PAD), lambda bi, c: (bi, c, 0)),
                  full((CONV_W, CONV_DIM)), full((1, CONV_DIM)), full((1, DT_PAD)), full((1, DT_PAD)),
                  full((1, D_INNER)), full((1, D_INNER)), full((LANES, D_INNER)), full((chunk, chunk))],
        out_specs=[pl.BlockSpec((1, chunk, D_INNER), lambda bi, c: (bi, c, 0)),
                   pl.BlockSpec((1, D_INNER, D_STATE), lambda bi, c: (bi, 0, 0))],
        out_shape=[jax.ShapeDtypeStruct((b, L, D_INNER), BF16),
                   jax.ShapeDtypeStruct((b, D_INNER, D_STATE), F32)],
        scratch_shapes=[pltpu.VMEM((SUBLANES + chunk, CONV_DIM), F32),
                        pltpu.VMEM((D_STATE, D_INNER), F32),
                        pltpu.VMEM((chunk, D_INNER), F32)],
        compiler_params=pltpu.CompilerParams(
            dimension_semantics=("arbitrary", "arbitrary"), vmem_limit_bytes=VMEM_LIMIT),
        name="ssd_prompt",
    )(xbc, z, dt, conv_w, conv_b, dtb_pad, alog_pad, dskip_exp, nw, e01, tri)


def _ssd_step_kernel(xbc_ref, z_ref, dt_ref, conv_ref, ssm_ref, cw_ref, cb_ref, dtb_ref, alog_ref, dskip_ref,
                     nw_ref, e_ref, ys_ref, conv_out_ref, ssm_out_ref):
    xn = xbc_ref[0]
    acc = cb_ref[...] + xn * cw_ref[CONV_W - 1:CONV_W, :]
    for jj in range(CONV_W - 1):
        acc = acc + conv_ref[0, jj:jj + 1, :] * cw_ref[jj:jj + 1, :]
        if jj > 0:
            conv_out_ref[0, jj - 1:jj, :] = conv_ref[0, jj:jj + 1, :]
    u = _silu(acc)
    conv_out_ref[0, CONV_W - 2:CONV_W - 1, :] = xn

    xs = u[:, :D_INNER]
    e01 = e_ref[...]
    dt = _softplus(dt_ref[0] + dtb_ref[...])
    d_a = jnp.exp(dt * (-jnp.exp(alog_ref[...])))
    rows8 = lax.broadcasted_iota(jnp.int32, (SUBLANES, DT_PAD), 0)
    two = jnp.where(rows8 == 0, dt, jnp.where(rows8 == 1, d_a, 0.0))
    two_exp = _exact_right(two, e01)
    xdt = xs * two_exp[0:1, :]
    rows = lax.broadcasted_iota(jnp.int32, (LANES, D_INNER), 0)
    stack = jnp.where(rows == 0, xdt, jnp.where(rows == 1, two_exp[1:2, :], 0.0))
    rows_c = lax.broadcasted_iota(jnp.int32, (SUBLANES, D_STATE), 0)
    c8 = jnp.zeros((SUBLANES, D_STATE), F32)
    for g in range(N_GROUPS):
        c_g = u[:, D_INNER + (N_GROUPS + g) * D_STATE:D_INNER + (N_GROUPS + g + 1) * D_STATE]
        c8 = jnp.where(rows_c == g, c_g, c8)
    c8 = c8.astype(BF16)
    blocks_per_group = D_INNER // N_GROUPS // LANES
    y_blocks = []
    for cbk in range(D_INNER // LANES):
        g = cbk // blocks_per_group
        cols = stack[:, cbk * LANES:(cbk + 1) * LANES].T
        b_row = u[:, D_INNER + g * D_STATE:D_INNER + (g + 1) * D_STATE]
        s_old = ssm_ref[0, cbk * LANES:(cbk + 1) * LANES, :]
        s_new = cols[:, 1:2] * s_old + cols[:, 0:1] * b_row
        ssm_out_ref[0, cbk * LANES:(cbk + 1) * LANES, :] = s_new
        yg = _dot_nt(c8, s_new.astype(BF16))
        y_blocks.append(yg[g:g + 1, :])
    y = jnp.concatenate(y_blocks, axis=1) + dskip_ref[...] * xs
    gw = D_INNER // N_GROUPS
    for g, o in enumerate(_gate_and_group_norm(y, z_ref[0], nw_ref[...])):
        ys_ref[0, :, g * gw:(g + 1) * gw] = o.astype(ys_ref.dtype)


def _ssd_step(xbc, z, dt, state_conv, state_ssm, conv_w, conv_b, dtb_pad, alog_pad, dskip_exp, nw, e01):
    nseq = xbc.shape[0]
    full = lambda shape: pl.BlockSpec(shape, lambda s: tuple(0 for _ in shape))
    per = lambda shape: pl.BlockSpec((1,) + shape, lambda s: (s, 0, 0))
    return pl.pallas_call(
        _ssd_step_kernel,
        grid=(nseq,),
        in_specs=[per((1, CONV_DIM)), per((1, D_INNER)), per((1, DT_PAD)), per((CONV_W - 1, CONV_DIM)),
                  per((D_INNER, D_STATE)),
                  full((CONV_W, CONV_DIM)), full((1, CONV_DIM)), full((1, DT_PAD)), full((1, DT_PAD)),
                  full((1, D_INNER)), full((1, D_INNER)), full((LANES, D_INNER))],
        out_specs=[per((1, D_INNER)), per((CONV_W - 1, CONV_DIM)), per((D_INNER, D_STATE))],
        out_shape=[jax.ShapeDtypeStruct((nseq, 1, D_INNER), BF16),
                   jax.ShapeDtypeStruct((nseq, CONV_W - 1, CONV_DIM), F32),
                   jax.ShapeDtypeStruct((nseq, D_INNER, D_STATE), F32)],
        compiler_params=pltpu.CompilerParams(dimension_semantics=("arbitrary",), vmem_limit_bytes=VMEM_LIMIT),
        name="ssd_step",
    )(xbc, z, dt, state_conv, state_ssm, conv_w, conv_b, dtb_pad, alog_pad, dskip_exp, nw, e01)


def _merge_kernel(x_ref, o_ref, ys_ref, g_ref, g1_ref, wa_ref, wb_ref, wo_ref, x1_ref):
    ya = _dot(o_ref[...], wa_ref[...])
    yb = _dot(ys_ref[...], wb_ref[...])
    gates = _sigmoid(g_ref[...])
    mix = gates[:, :D_MODEL] * ya + gates[:, D_MODEL:] * yb
    mixed = _dot(mix.astype(BF16), wo_ref[...])
    x1_ref[...] = x_ref[...] + g1_ref[0] * mixed


def _merge(x2d, o2d, ys2d, gate2d, g1, wa, wb, wo, tm, rows_per_mod):
    rows = x2d.shape[0]
    mod_rows = g1.shape[1]
    tiles_per_mod = rows_per_mod // tm
    row_blk = lambda width: pl.BlockSpec((tm, width), lambda i: (i, 0))
    wfull = lambda shape: pl.BlockSpec(shape, lambda i: (0, 0), pipeline_mode=pl.Buffered(1))
    return pl.pallas_call(
        _merge_kernel,
        grid=(rows // tm,),
        in_specs=[row_blk(D_MODEL), row_blk(ATTN_WIDTH), row_blk(D_INNER), row_blk(GATE_WIDTH),
                  pl.BlockSpec((1, mod_rows, D_MODEL), lambda i: (i // tiles_per_mod, 0, 0)),
                  wfull((ATTN_WIDTH, D_MODEL)), wfull((D_INNER, D_MODEL)), wfull((D_MODEL, D_MODEL))],
        out_specs=row_blk(D_MODEL),
        out_shape=jax.ShapeDtypeStruct((rows, D_MODEL), F32),
        compiler_params=pltpu.CompilerParams(dimension_semantics=("arbitrary",), vmem_limit_bytes=VMEM_LIMIT),
        name="merge",
    )(x2d, o2d, ys2d, gate2d, g1, wa, wb, wo)


def _ffn_kernel(x_ref, sh_ref, sc_ref, g2_ref, nw_ref, fw_ref, wg_ref, wu_ref, wd_ref, y_ref):
    x = x_ref[...]
    ms = jnp.mean(x * x, axis=-1, keepdims=True)
    h = x * lax.rsqrt(ms + EPS) * nw_ref[...]
    hb = (h * (1.0 + sc_ref[0]) + sh_ref[0]).astype(BF16)
    ff = _silu(_dot(hb, wg_ref[...])) * _dot(hb, wu_ref[...])
    x2 = x + g2_ref[0] * _dot(ff.astype(BF16), wd_ref[...])
    ms2 = jnp.mean(x2 * x2, axis=-1, keepdims=True)
    y_ref[...] = x2 * lax.rsqrt(ms2 + EPS) * fw_ref[...]


def _ffn(x2d, sh, sc, g2, nw, fw, wg, wu, wd, tm, rows_per_mod):
    rows = x2d.shape[0]
    mod_rows = sh.shape[1]
    tiles_per_mod = rows_per_mod // tm
    row_blk = pl.BlockSpec((tm, D_MODEL), lambda i: (i, 0))
    mod_blk = pl.BlockSpec((1, mod_rows, D_MODEL), lambda i: (i // tiles_per_mod, 0, 0))
    vec = pl.BlockSpec((1, D_MODEL), lambda i: (0, 0))
    wfull = lambda shape: pl.BlockSpec(shape, lambda i: (0, 0), pipeline_mode=pl.Buffered(1))
    return pl.pallas_call(
        _ffn_kernel,
        grid=(rows // tm,),
        in_specs=[row_blk, mod_blk, mod_blk, mod_blk, vec, vec,
                  wfull((D_MODEL, D_FF)), wfull((D_MODEL, D_FF)), wfull((D_FF, D_MODEL))],
        out_specs=row_blk,
        out_shape=jax.ShapeDtypeStruct((rows, D_MODEL), F32),
        compiler_params=pltpu.CompilerParams(dimension_semantics=("arbitrary",), vmem_limit_bytes=VMEM_LIMIT),
        name="ffn",
    )(x2d, sh, sc, g2, nw, fw, wg, wu, wd)


def _layer(x2d, mods, pos, attend, mamba, p, tm_proj, tm_out, rows_per_mod, with_vt):
    sh1, sc1, g1, sh2, sc2, g2 = mods
    ct, s1, s2 = _rope_tables(pos)
    qb, k, kb, v, z, xbc, gate, dt, *maybe_vt = _in_proj(x2d, sh1, sc1, p['norm1_w'], p['w_in'], ct, s1, s2,
                                                         tm_proj, rows_per_mod, with_vt)
    o = attend(qb, k, kb, v, *maybe_vt)
    ys, new_conv, new_ssm = mamba(z, xbc, dt)
    x1 = _merge(x2d, o, ys, gate, g1, p['w_branch_attn'], p['w_branch_ssm'], p['w_out'], tm_out, rows_per_mod)
    y = _ffn(x1, sh2, sc2, g2, p['norm2_w'], p['final_norm_w'], p['w_ffn_gate'], p['w_ffn_up'], p['w_ffn_down'],
             tm_out, rows_per_mod)
    return y, k, v, new_conv, new_ssm


def kernel(x_prompt, x_sample, c_prompt, c_sample, cache_k, cache_v, page_table, state_conv, state_ssm, w_ada, b_ada, norm1_w, w_in, lambda_q1, lambda_k1, lambda_q2, lambda_k2, subln_w, conv_w, conv_b, dt_bias, a_log, d_skip, ssm_norm_w, w_branch_attn, w_branch_ssm, w_out, norm2_w, w_ffn_gate, w_ffn_up, w_ffn_down, final_norm_w):
    depth = w_in.shape[0]
    assert depth == 1
    bp, L, d = x_prompt.shape
    ns, Ld, _ = x_sample.shape
    assert Ld == 1 and d == D_MODEL
    n_pages = page_table.shape[1]
    past = n_pages * PAGE_SIZE
    lam_init = 0.8 - 0.6 * math.exp(-0.3 * 0)
    i = 0

    splits = np.cumsum([QK_WIDTH, QK_WIDTH, ATTN_WIDTH, D_INNER, CONV_DIM, SSM_HEADS])
    wq, wk, wv, wz, wx, wdt, wg = jnp.split(w_in[i], splits, axis=1)
    w_in_r = jnp.concatenate(
        [wq, wk, wv, wz, wx, wg, wdt, jnp.zeros((D_MODEL, DT_PAD - SSM_HEADS), F32)], axis=1).astype(BF16)
    pad_heads = lambda t: jnp.concatenate([t, jnp.zeros((DT_PAD - SSM_HEADS,), F32)]).reshape(1, DT_PAD)
    e01 = (jnp.arange(LANES)[:, None] == (jnp.arange(D_INNER)[None, :] // SSM_HEAD_DIM)).astype(BF16)
    p = {
        'norm1_w': norm1_w[i].reshape(1, -1), 'w_in': w_in_r,
        'w_branch_attn': w_branch_attn[i].astype(BF16), 'w_branch_ssm': w_branch_ssm[i].astype(BF16),
        'w_out': w_out[i].astype(BF16), 'norm2_w': norm2_w[i].reshape(1, -1),
        'final_norm_w': final_norm_w.reshape(1, -1),
        'w_ffn_gate': w_ffn_gate[i].astype(BF16), 'w_ffn_up': w_ffn_up[i].astype(BF16),
        'w_ffn_down': w_ffn_down[i].astype(BF16),
    }
    lam_vecs = [t[i].reshape(1, HEAD_DIM) for t in (lambda_q1, lambda_k1, lambda_q2, lambda_k2)]
    sw = subln_w[i].reshape(1, V_DIM)
    mamba_params = (conv_w[i], conv_b[i].reshape(1, -1), pad_heads(dt_bias[i]), pad_heads(a_log[i]),
                    jnp.repeat(d_skip[i], SSM_HEAD_DIM).reshape(1, D_INNER), ssm_norm_w[i].reshape(1, -1), e01)

    n_c = bp + ns
    c_rows = -(-n_c // SUBLANES) * SUBLANES
    c_all = jnp.concatenate([c_prompt, c_sample, jnp.zeros((c_rows - n_c, d), F32)], axis=0)
    mod = _ada_mod(c_all, w_ada[i], b_ada[i])
    mods_p = [m.reshape(bp, 1, d) for m in jnp.split(mod[:bp], 6, axis=1)]
    mods_s = [m.reshape(1, ns, d) for m in jnp.split(mod[bp:bp + ns], 6, axis=1)]

    chunk = 128
    tri = jnp.tril(jnp.ones((chunk, chunk), F32)).astype(BF16)

    def attend_p(qb, k, kb, v, vt):
        r3 = lambda t: t.reshape(bp, L, -1)
        o = _prompt_attention(r3(qb), r3(kb), vt, *lam_vecs, sw.reshape(V_DIM, 1), lam_init, tq=512)
        return o.reshape(bp * L, ATTN_WIDTH)

    def mamba_p(z, xbc, dt):
        xbc3 = xbc.reshape(bp, L, CONV_DIM)
        ys, ssm = _ssd_prompt(xbc3, z.reshape(bp, L, D_INNER), dt.reshape(bp, L, DT_PAD),
                              *mamba_params[:6], e01, tri, chunk)
        return ys.reshape(bp * L, D_INNER), xbc3[:, L - (CONV_W - 1):, :], ssm

    yp, kp, vp, cp, sp = _layer(x_prompt.reshape(bp * L, d), mods_p, jnp.arange(L), attend_p, mamba_p, p,
                                tm_proj=256, tm_out=512, rows_per_mod=L, with_vt=True)

    pool = cache_k.shape[1]
    ck = cache_k[i].reshape(pool, PAGE_SIZE * N_HEADS, V_DIM)
    cv = cache_v[i].reshape(pool, PAGE_SIZE * N_HEADS, V_DIM)

    def attend_s(qb, k, kb, v):
        qh = qb.reshape(ns, N_HEADS, V_DIM)
        lane = jnp.arange(V_DIM)[None, None, :]
        q16 = jnp.concatenate([jnp.where(lane < HEAD_DIM, qh, 0), jnp.where(lane >= HEAD_DIM, qh, 0)], axis=1)
        padn = lambda t: jnp.concatenate(
            [t.reshape(ns, N_HEADS, V_DIM), jnp.zeros((ns, LANES - N_HEADS, V_DIM), F32)], axis=1)
        o = _sample_attention(q16, padn(k), padn(v), ck, cv, page_table, *lam_vecs, sw, lam_init,
                              pages=16, group=4)
        return o.reshape(ns, ATTN_WIDTH)

    def mamba_s(z, xbc, dt):
        ys, conv, ssm = _ssd_step(xbc.reshape(ns, 1, CONV_DIM), z.reshape(ns, 1, D_INNER), dt.reshape(ns, 1, DT_PAD),
                                  state_conv[i], state_ssm[i].reshape(ns, D_INNER, D_STATE), *mamba_params)
        return ys.reshape(ns, D_INNER), conv, ssm

    ys_, ks, vs, cs, ss = _layer(x_sample.reshape(ns, d), mods_s, jnp.full((ns,), past), attend_s, mamba_s, p,
                                 tm_proj=ns, tm_out=ns, rows_per_mod=ns, with_vt=False)

    hk = (N_HEADS, 2 * HEAD_DIM)
    return (yp.reshape(bp, L, d), ys_.reshape(ns, 1, d),
            kp.reshape((1, bp, L) + hk), vp.reshape((1, bp, L) + hk),
            cp.reshape(1, bp, CONV_W - 1, CONV_DIM), sp.reshape(1, bp, SSM_HEADS, SSM_HEAD_DIM, D_STATE),
            ks.reshape((1, ns, 1) + hk), vs.reshape((1, ns, 1) + hk),
            cs.reshape(1, ns, CONV_W - 1, CONV_DIM), ss.reshape(1, ns, SSM_HEADS, SSM_HEAD_DIM, D_STATE))
```

```python
import functools
import math

import jax
import jax.numpy as jnp
import numpy as np
from jax import lax
from jax.experimental import pallas as pl
from jax.experimental.pallas import tpu as pltpu

F32 = jnp.float32
BF16 = jnp.bfloat16

D_MODEL = 1024
N_HEADS = 8
HEAD_DIM = 64
V_DIM = 2 * HEAD_DIM
QK_WIDTH = N_HEADS * 2 * HEAD_DIM
ATTN_WIDTH = N_HEADS * V_DIM
ROPE_DIM = HEAD_DIM // 4
ROPE_THETA = 500000.0
D_INNER = 2 * D_MODEL
SSM_HEAD_DIM = 64
SSM_HEADS = D_INNER // SSM_HEAD_DIM
N_GROUPS = 4
HEADS_PER_GROUP = SSM_HEADS // N_GROUPS
D_STATE = 128
CONV_W = 4
CONV_DIM = D_INNER + 2 * N_GROUPS * D_STATE
D_FF = -(-8 * D_MODEL // (3 * 256)) * 256
GATE_WIDTH = 2 * D_MODEL
PAGE_SIZE = 128
EPS = 1e-6

LANES = 128
SUBLANES = 8
DT_PAD = LANES
IN_WIDTH_PAD = 2 * QK_WIDTH + ATTN_WIDTH + D_INNER + CONV_DIM + GATE_WIDTH + DT_PAD
VMEM_LIMIT = 56 * 1024 * 1024

_OQ = 0
_OK = _OQ + QK_WIDTH
_OV = _OK + QK_WIDTH
_OZ = _OV + ATTN_WIDTH
_OX = _OZ + D_INNER
_OG = _OX + CONV_DIM
_OD = _OG + GATE_WIDTH


def _silu(x):
    return x * (1.0 / (1.0 + jnp.exp(-x)))


def _sigmoid(x):
    return 1.0 / (1.0 + jnp.exp(-x))


def _softplus(x):
    return jnp.maximum(x, 0.0) + jnp.log(1.0 + jnp.exp(-jnp.abs(x)))


def _split3(a):
    hi = a.astype(BF16)
    r1 = a - hi.astype(F32)
    mid = r1.astype(BF16)
    lo = (r1 - mid.astype(F32)).astype(BF16)
    return hi, mid, lo


def _dot(a, b):
    return jnp.dot(a, b, preferred_element_type=F32)


def _dot_nt(a, b):
    return lax.dot_general(a, b, (((1,), (1,)), ((), ())), preferred_element_type=F32)


def _exact_right(a, e01):
    hi, mid, lo = _split3(a)
    return _dot(hi, e01) + _dot(mid, e01) + _dot(lo, e01)


def _exact_left(t01, a):
    hi, mid, lo = _split3(a)
    return _dot(t01, hi) + _dot(t01, mid) + _dot(t01, lo)


def _ada_kernel(c_ref, w_ref, b_ref, o_ref):
    c = _silu(c_ref[...]).astype(BF16)
    o_ref[...] = _dot(c, w_ref[...].astype(BF16)) + b_ref[...]


def _ada_mod(c_all, w_ada, b_ada):
    rows = c_all.shape[0]
    tn = D_MODEL
    return pl.pallas_call(
        _ada_kernel,
        grid=(6 * D_MODEL // tn,),
        in_specs=[pl.BlockSpec((rows, D_MODEL), lambda j: (0, 0)),
                  pl.BlockSpec((D_MODEL, tn), lambda j: (0, j)),
                  pl.BlockSpec((1, tn), lambda j: (0, j))],
        out_specs=pl.BlockSpec((rows, tn), lambda j: (0, j)),
        out_shape=jax.ShapeDtypeStruct((rows, 6 * D_MODEL), F32),
        name="ada_mod",
    )(c_all, w_ada, b_ada.reshape(1, -1))


def _rope_cols(t, c_tab, s1_tab, s2_tab, scale):
    outs = []
    for cb in range(t.shape[1] // LANES):
        tc = t[:, cb * LANES:(cb + 1) * LANES]
        up = pltpu.roll(tc, LANES - ROPE_DIM // 2, axis=1)
        dn = pltpu.roll(tc, ROPE_DIM // 2, axis=1)
        r = tc * c_tab + up * s1_tab + dn * s2_tab
        outs.append(r * scale if scale != 1.0 else r)
    return outs


def _inproj_kernel(x_ref, sh_ref, sc_ref, nw_ref, w_ref, ct_ref, s1_ref, s2_ref,
                   q_ref, k_ref, kb_ref, v_ref, z_ref, xbc_ref, g_ref, dt_ref, *maybe_vt_ref):
    x = x_ref[...]
    ms = jnp.mean(x * x, axis=-1, keepdims=True)
    h = x * lax.rsqrt(ms + EPS) * nw_ref[...]
    h = h * (1.0 + sc_ref[0]) + sh_ref[0]
    hb = h.astype(BF16)

    def proj(lo, width):
        return _dot(hb, w_ref[:, lo:lo + width])

    ct, s1, s2 = ct_ref[...], s1_ref[...], s2_ref[...]
    q = _rope_cols(proj(_OQ, QK_WIDTH), ct, s1, s2, math.log2(math.e) / math.sqrt(HEAD_DIM))
    for cb, r in enumerate(q):
        q_ref[:, cb * LANES:(cb + 1) * LANES] = r.astype(BF16)
    k = _rope_cols(proj(_OK, QK_WIDTH), ct, s1, s2, 1.0)
    for cb, r in enumerate(k):
        k_ref[:, cb * LANES:(cb + 1) * LANES] = r
        kb_ref[:, cb * LANES:(cb + 1) * LANES] = r.astype(BF16)
    v = proj(_OV, ATTN_WIDTH)
    v_ref[...] = v
    for vt_ref in maybe_vt_ref:
        for cb in range(ATTN_WIDTH // LANES):
            vt_ref[0, cb * LANES:(cb + 1) * LANES, :] = v[:, cb * LANES:(cb + 1) * LANES].T.astype(BF16)
    z_ref[...] = proj(_OZ, D_INNER)
    xbc_ref[...] = proj(_OX, CONV_DIM)
    g_ref[...] = proj(_OG, GATE_WIDTH)
    dt_ref[...] = proj(_OD, DT_PAD)


def _in_proj(x2d, sh, sc, nw, w_in, ctab, s1tab, s2tab, tm, rows_per_mod, with_vt):
    rows = x2d.shape[0]
    mod_rows = sh.shape[1]
    tiles_per_mod = rows_per_mod // tm
    row_blk = lambda width: pl.BlockSpec((tm, width), lambda i: (i, 0))
    mod_blk = pl.BlockSpec((1, mod_rows, D_MODEL), lambda i: (i // tiles_per_mod, 0, 0))
    assert ctab.shape[0] == rows_per_mod
    tab_blk = pl.BlockSpec((tm, LANES), lambda i: (i % tiles_per_mod, 0))
    outs = [(QK_WIDTH, BF16), (QK_WIDTH, F32), (QK_WIDTH, BF16), (ATTN_WIDTH, F32),
            (D_INNER, F32), (CONV_DIM, F32), (GATE_WIDTH, F32), (DT_PAD, F32)]
    out_specs = [row_blk(w) for w, _ in outs]
    out_shape = [jax.ShapeDtypeStruct((rows, w), dt) for w, dt in outs]
    if with_vt:
        out_specs.append(pl.BlockSpec((1, ATTN_WIDTH, tm), lambda i: (i, 0, 0)))
        out_shape.append(jax.ShapeDtypeStruct((rows // tm, ATTN_WIDTH, tm), BF16))
    return pl.pallas_call(
        _inproj_kernel,
        grid=(rows // tm,),
        in_specs=[row_blk(D_MODEL), mod_blk, mod_blk,
                  pl.BlockSpec((1, D_MODEL), lambda i: (0, 0)),
                  pl.BlockSpec((D_MODEL, IN_WIDTH_PAD), lambda i: (0, 0), pipeline_mode=pl.Buffered(1)),
                  tab_blk, tab_blk, tab_blk],
        out_specs=out_specs,
        out_shape=out_shape,
        compiler_params=pltpu.CompilerParams(dimension_semantics=("arbitrary",), vmem_limit_bytes=VMEM_LIMIT),
        name="in_proj",
    )(x2d, sh, sc, nw, w_in, ctab, s1tab, s2tab)


def _rope_tables(pos):
    half = ROPE_DIM // 2
    inv_freq = ROPE_THETA ** (-jnp.arange(half, dtype=F32) * 2.0 / ROPE_DIM)
    ang = pos.astype(F32)[:, None] * inv_freq[None, :]
    cos, sin = jnp.cos(ang), jnp.sin(ang)
    n = pos.shape[0]
    rest = HEAD_DIM - ROPE_DIM
    c64 = jnp.concatenate([cos, cos, jnp.ones((n, rest), F32)], axis=1)
    s1_64 = jnp.concatenate([-sin, jnp.zeros((n, half + rest), F32)], axis=1)
    s2_64 = jnp.concatenate([jnp.zeros((n, half), F32), sin, jnp.zeros((n, rest), F32)], axis=1)
    rep = LANES // HEAD_DIM
    return jnp.tile(c64, (1, rep)), jnp.tile(s1_64, (1, rep)), jnp.tile(s2_64, (1, rep))


def _lambda_value(lq1, lk1, lq2, lk2, lam_init):
    a = jnp.sum(lq1[...] * lk1[...], axis=1, keepdims=True)
    b = jnp.sum(lq2[...] * lk2[...], axis=1, keepdims=True)
    return jnp.exp(a) - jnp.exp(b) + lam_init


def _prompt_attn_part(qi, q_ref, k_ref, vt_ref, lam, sw_ref, o_ref,
                      qm_ref, sa_ref, sb_ref, m_ref, acc_ref, *, tq, tk, lam_init):
    assert tq == 2 * tk
    q = q_ref[0]
    lane = lax.broadcasted_iota(jnp.int32, q.shape, 1)
    zero = jnp.zeros_like(q)
    qm_ref[0] = jnp.where(lane < HEAD_DIM, q, zero)
    qm_ref[1] = jnp.where(lane >= HEAD_DIM, q, zero)
    m_ref[...] = jnp.full(m_ref.shape, -jnp.inf, F32)
    acc_ref[...] = jnp.zeros(acc_ref.shape, F32)

    def scores(ki, s_ref):
        k = k_ref[0, pl.ds(pl.multiple_of(ki * tk, tk), tk), :]
        for mp in range(2):
            s_ref[mp] = _dot_nt(k, qm_ref[mp])

    def absorb(ki, s_ref, diag):
        vt = jnp.concatenate([vt_ref[ki], jnp.ones((acc_ref.shape[1] - V_DIM, tk), BF16)], axis=0)
        for mp in range(2):
            st = s_ref[mp]
            if diag is not None:
                key = diag * tk + lax.broadcasted_iota(jnp.int32, st.shape, 0)
                qry = lax.broadcasted_iota(jnp.int32, st.shape, 1)
                st = jnp.where(key <= qry, st, -jnp.inf)
            m_prev = m_ref[mp]
            m_new = jnp.maximum(m_prev, jnp.max(st, axis=0, keepdims=True))
            alpha = jnp.exp2(m_prev - m_new)
            pt = jnp.exp2(st - m_new)
            acc_ref[mp] = alpha * acc_ref[mp] + _dot(vt, pt.astype(BF16))
            m_ref[mp] = m_new

    n_full = 2 * qi
    scores(0, sa_ref)

    def body(j, carry):
        scores(2 * j + 1, sb_ref)
        absorb(2 * j, sa_ref, None)
        scores(2 * j + 2, sa_ref)
        absorb(2 * j + 1, sb_ref, None)
        return carry

    lax.fori_loop(0, qi, body, 0)
    scores(n_full + 1, sb_ref)
    absorb(n_full, sa_ref, 0)
    absorb(n_full + 1, sb_ref, 1)

    o1 = acc_ref[0, 0:V_DIM, :] / acc_ref[0, V_DIM:V_DIM + 1, :]
    o2 = acc_ref[1, 0:V_DIM, :] / acc_ref[1, V_DIM:V_DIM + 1, :]
    ot = o1 - lam * o2
    ms = jnp.mean(ot * ot, axis=0, keepdims=True)
    ot = ot * lax.rsqrt(ms + EPS) * sw_ref[...] * (1.0 - lam_init)
    o_ref[0] = ot.T.astype(o_ref.dtype)


def _sample_attn_part(j, n_j, q_ref, kn_ref, vn_ref, k_refs, v_refs, lam, sw_ref, o_ref,
                      m_ref, l_ref, acc_ref, *, group, lam_init):
    pages = len(k_refs)
    nrow = 2 * N_HEADS
    q16 = q_ref[0]

    def head_mask(nkeys):
        row = lax.broadcasted_iota(jnp.int32, (nrow, nkeys), 0)
        col = lax.broadcasted_iota(jnp.int32, (nrow, nkeys), 1)
        return (row % N_HEADS) == (col % N_HEADS)

    def scores(ks, valid):
        return [jnp.where(valid, _dot(q16, k2d.astype(BF16).T), -jnp.inf) for k2d in ks]

    def absorb(ss, vs):
        m_blk = jnp.max(ss[0], axis=1, keepdims=True)
        for s in ss[1:]:
            m_blk = jnp.maximum(m_blk, jnp.max(s, axis=1, keepdims=True))
        m_prev = m_ref[...]
        m_new = jnp.maximum(m_prev, m_blk)
        alpha = jnp.exp2(m_prev - m_new)
        l_new = alpha * l_ref[...]
        acc = alpha * acc_ref[...]
        for s, v2d in zip(ss, vs):
            p = jnp.exp2(s - m_new)
            l_new = l_new + jnp.sum(p, axis=1, keepdims=True)
            acc = acc + _dot(p.astype(BF16), v2d)
        l_ref[...] = l_new
        acc_ref[...] = acc
        m_ref[...] = m_new

    @pl.when(j == 0)
    def _():
        m_ref[...] = jnp.full(m_ref.shape, -jnp.inf, F32)
        l_ref[...] = jnp.zeros(l_ref.shape, F32)
        acc_ref[...] = jnp.zeros(acc_ref.shape, F32)
        col = lax.broadcasted_iota(jnp.int32, (nrow, LANES), 1)
        absorb(scores([kn_ref[0]], head_mask(LANES) & (col < N_HEADS)), [vn_ref[0].astype(BF16)])

    valid = head_mask(PAGE_SIZE * N_HEADS)
    bounds = list(range(0, pages + 1, group))
    ss = scores([r[0] for r in k_refs[0:group]], valid)
    for lo, hi in zip(bounds[:-1], bounds[1:]):
        nxt = scores([r[0] for r in k_refs[hi:hi + group]], valid) if hi < pages else None
        absorb(ss, [r[0].astype(BF16) for r in v_refs[lo:hi]])
        ss = nxt

    @pl.when(j == n_j - 1)
    def _():
        o1 = acc_ref[0:N_HEADS, :] / l_ref[0:N_HEADS, :]
        o2 = acc_ref[N_HEADS:nrow, :] / l_ref[N_HEADS:nrow, :]
        o = o1 - lam * o2
        ms = jnp.mean(o * o, axis=-1, keepdims=True)
        o_ref[0] = (o * lax.rsqrt(ms + EPS) * sw_ref[...] * (1.0 - lam_init)).astype(o_ref.dtype)


def _attn_kernel(pt_ref, q_ref, k_ref, vt_ref, qs_ref, kn_ref, vn_ref, *rest, pages, group, steps_per_seq,
                 tq, tk, lam_init):
    k_refs = rest[:pages]
    v_refs = rest[pages:2 * pages]
    (lq1, lk1, lq2, lk2, sw_col_ref, sw_row_ref, o_ref, os_ref,
     qm_ref, sa_ref, sb_ref, m_ref, acc_ref, ms_ref, ls_ref, accs_ref) = rest[2 * pages:]
    step = (pl.program_id(0) * pl.num_programs(1) + pl.program_id(1)) * pl.num_programs(2) + pl.program_id(2)
    lam = _lambda_value(lq1, lk1, lq2, lk2, lam_init)
    _sample_attn_part(step % steps_per_seq, steps_per_seq, qs_ref, kn_ref, vn_ref, k_refs, v_refs, lam,
                      sw_row_ref, os_ref, ms_ref, ls_ref, accs_ref, group=group, lam_init=lam_init)
    _prompt_attn_part(pl.program_id(2), q_ref, k_ref, vt_ref, lam, sw_col_ref, o_ref,
                      qm_ref, sa_ref, sb_ref, m_ref, acc_ref, tq=tq, tk=tk, lam_init=lam_init)


def _attention(qb, kb, vt, q16, kn_pad, vn_pad, cache_k, cache_v, page_table, lq1, lk1, lq2, lk2, subln_w,
               lam_init, tq, pages, group):
    b, L, _ = qb.shape
    tk = vt.shape[2]
    nq = L // tq
    nseq, n_pages = page_table.shape
    assert n_pages % pages == 0 and pages % group == 0
    steps_per_seq = n_pages // pages
    assert b * N_HEADS * nq == nseq * steps_per_seq

    def seq_of(bi, h, qi):
        return ((bi * N_HEADS + h) * nq + qi) // steps_per_seq

    def page_blk(i):
        def idx(bi, h, qi, pt):
            step = (bi * N_HEADS + h) * nq + qi
            return (pt[step // steps_per_seq, (step % steps_per_seq) * pages + i], 0, 0)
        return pl.BlockSpec((1, PAGE_SIZE * N_HEADS, V_DIM), idx)

    const = lambda shape: pl.BlockSpec(shape, lambda bi, h, qi, pt: (0, 0))
    per_seq = lambda rows: pl.BlockSpec((1, rows, V_DIM), lambda bi, h, qi, pt: (seq_of(bi, h, qi), 0, 0))
    q_blk = pl.BlockSpec((1, tq, V_DIM), lambda bi, h, qi, pt: (bi, qi, h))
    vec = const((1, HEAD_DIM))
    grid_spec = pltpu.PrefetchScalarGridSpec(
        num_scalar_prefetch=1,
        grid=(b, N_HEADS, nq),
        in_specs=[q_blk,
                  pl.BlockSpec((1, L, V_DIM), lambda bi, h, qi, pt: (bi, 0, h)),
                  pl.BlockSpec((L // tk, V_DIM, tk), lambda bi, h, qi, pt: (bi, h, 0)),
                  per_seq(2 * N_HEADS), per_seq(LANES), per_seq(LANES)]
                 + [page_blk(i) for i in range(pages)] + [page_blk(i) for i in range(pages)]
                 + [vec, vec, vec, vec, const((V_DIM, 1)), const((1, V_DIM))],
        out_specs=[q_blk, per_seq(N_HEADS)],
        scratch_shapes=[pltpu.VMEM((2, tq, V_DIM), BF16),
                        pltpu.VMEM((2, tk, tq), F32),
                        pltpu.VMEM((2, tk, tq), F32),
                        pltpu.VMEM((2, 1, tq), F32),
                        pltpu.VMEM((2, V_DIM + 16, tq), F32),
                        pltpu.VMEM((2 * N_HEADS, 1), F32),
                        pltpu.VMEM((2 * N_HEADS, 1), F32),
                        pltpu.VMEM((2 * N_HEADS, V_DIM), F32)],
    )
    return pl.pallas_call(
        functools.partial(_attn_kernel, pages=pages, group=group, steps_per_seq=steps_per_seq,
                          tq=tq, tk=tk, lam_init=lam_init),
        grid_spec=grid_spec,
        out_shape=[jax.ShapeDtypeStruct((b, L, ATTN_WIDTH), BF16),
                   jax.ShapeDtypeStruct((nseq, N_HEADS, V_DIM), BF16)],
        compiler_params=pltpu.CompilerParams(
            dimension_semantics=("arbitrary", "arbitrary", "arbitrary"), vmem_limit_bytes=VMEM_LIMIT),
        name="attn",
    )(page_table, qb, kb, vt, q16, kn_pad, vn_pad, *([cache_k] * pages), *([cache_v] * pages),
      lq1, lk1, lq2, lk2, subln_w.reshape(V_DIM, 1), subln_w.reshape(1, V_DIM))


def _gate_and_group_norm(y, z, nw):
    y = y * _silu(z)
    gw = D_INNER // N_GROUPS
    outs = []
    for g in range(N_GROUPS):
        yg = y[:, g * gw:(g + 1) * gw]
        ms = jnp.mean(yg * yg, axis=-1, keepdims=True)
        outs.append(yg * lax.rsqrt(ms + EPS) * nw[:, g * gw:(g + 1) * gw])
    return outs


def _ssd_prompt_kernel(xbc_ref, z_ref, dt_ref, cw_ref, cb_ref, dtb_ref, alog_ref, dskip_ref, nw_ref,
                       e_ref, tri_ref, ys_ref, ssm_ref, ext_ref, st_ref, y_ref, *, chunk):
    c = pl.program_id(1)
    halo = SUBLANES

    @pl.when(c == 0)
    def _():
        ext_ref[0:halo, :] = jnp.zeros((halo, CONV_DIM), F32)
        st_ref[...] = jnp.zeros(st_ref.shape, F32)

    xt = xbc_ref[0]
    ext_ref[halo:halo + chunk, :] = xt
    acc = cb_ref[...] + ext_ref[halo - 3:halo - 3 + chunk, :] * cw_ref[0:1, :]
    for jj in range(1, CONV_W):
        acc = acc + ext_ref[halo - 3 + jj:halo - 3 + jj + chunk, :] * cw_ref[jj:jj + 1, :]
    u = _silu(acc)
    ext_ref[0:halo, :] = xt[chunk - halo:chunk, :]

    xs = u[:, :D_INNER]
    tri = tri_ref[...]
    e01 = e_ref[...]
    dt = _softplus(dt_ref[0] + dtb_ref[...])
    a = dt * (-jnp.exp(alog_ref[...]))
    a_cum = _exact_left(tri, a)
    a_cum_t = a_cum.T
    a_exp = _exact_right(a_cum, e01)
    dt_exp = _exact_right(dt, e01)
    ea = jnp.exp(a_exp)
    a_last = a_exp[chunk - 1:chunk, :]
    xdt = xs * dt_exp
    xw = (xdt * jnp.exp(a_last - a_exp)).astype(BF16)
    xdt_b = xdt.astype(BF16)
    causal = lax.broadcasted_iota(jnp.int32, (chunk, chunk), 0) >= lax.broadcasted_iota(jnp.int32, (chunk, chunk), 1)
    lane = lax.broadcasted_iota(jnp.int32, (chunk, LANES), 1)
    gw = D_INNER // N_GROUPS

    for g in range(N_GROUPS):
        bg = u[:, D_INNER + g * D_STATE:D_INNER + (g + 1) * D_STATE]
        cg = u[:, D_INNER + (N_GROUPS + g) * D_STATE:D_INNER + (N_GROUPS + g + 1) * D_STATE]
        bg_b, cg_b = bg.astype(BF16), cg.astype(BF16)
        cb = _dot_nt(cg_b, bg_b)
        st_g = st_ref[:, g * gw:(g + 1) * gw]
        y_off = _dot(cg_b, st_g.astype(BF16))
        s_new = _dot(bg.T.astype(BF16), xw[:, g * gw:(g + 1) * gw])
        st_ref[:, g * gw:(g + 1) * gw] = st_g * ea[chunk - 1:chunk, g * gw:(g + 1) * gw] + s_new
        for pp in range(HEADS_PER_GROUP // 2):
            col0 = g * gw + pp * LANES
            xpair = xdt_b[:, col0:col0 + LANES]
            ys_pair = []
            for hh in range(2):
                r = g * HEADS_PER_GROUP + 2 * pp + hh
                seg = a_cum[:, r:r + 1] - a_cum_t[r:r + 1, :]
                decay = jnp.exp(jnp.where(causal, seg, -jnp.inf))
                ys_pair.append(_dot((cb * decay).astype(BF16), xpair))
            y_diag = jnp.where(lane < SSM_HEAD_DIM, ys_pair[0], ys_pair[1])
            y_ref[:, col0:col0 + LANES] = y_diag + ea[:, col0:col0 + LANES] * y_off[:, pp * LANES:(pp + 1) * LANES]

    y = y_ref[...] + dskip_ref[...] * xs
    for g, o in enumerate(_gate_and_group_norm(y, z_ref[0], nw_ref[...])):
        ys_ref[0, :, g * gw:(g + 1) * gw] = o.astype(ys_ref.dtype)

    @pl.when(c == pl.num_programs(1) - 1)
    def _():
        ssm_ref[0] = st_ref[...].T


def _ssd_prompt(xbc, z, dt, conv_w, conv_b, dtb_pad, alog_pad, dskip_exp, nw, e01, tri, chunk):
    b, L, _ = xbc.shape
    full = lambda shape: pl.BlockSpec(shape, lambda bi, c: tuple(0 for _ in shape))
    return pl.pallas_call(
        functools.partial(_ssd_prompt_kernel, chunk=chunk),
        grid=(b, L // chunk),
        in_specs=[pl.BlockSpec((1, chunk, CONV_DIM), lambda bi, c: (bi, c, 0)),
                  pl.BlockSpec((1, chunk, D_INNER), lambda bi, c: (bi, c, 0)),
                  pl.BlockSpec((1, chunk, DT_PAD), lambda bi, c: (bi, c, 0)),
                  full((CONV_W, CONV_DIM)), full((1, CONV_DIM)), full((1, DT_PAD)), full((1, DT_PAD)),
                  full((1, D_INNER)), full((1, D_INNER)), full((LANES, D_INNER)), full((chunk, chunk))],
        out_specs=[pl.BlockSpec((1, chunk, D_INNER), lambda bi, c: (bi, c, 0)),
                   pl.BlockSpec((1, D_INNER, D_STATE), lambda bi, c: (bi, 0, 0))],
        out_shape=[jax.ShapeDtypeStruct((b, L, D_INNER), BF16),
                   jax.ShapeDtypeStruct((b, D_INNER, D_STATE), F32)],
        scratch_shapes=[pltpu.VMEM((SUBLANES + chunk, CONV_DIM), F32),
                        pltpu.VMEM((D_STATE, D_INNER), F32),
                        pltpu.VMEM((chunk, D_INNER), F32)],
        compiler_params=pltpu.CompilerParams(
            dimension_semantics=("arbitrary", "arbitrary"), vmem_limit_bytes=VMEM_LIMIT),
        name="ssd_prompt",
    )(xbc, z, dt, conv_w, conv_b, dtb_pad, alog_pad, dskip_exp, nw, e01, tri)


def _ssd_step_kernel(xbc_ref, z_ref, dt_ref, conv_ref, ssm_ref, cw_ref, cb_ref, dtb_ref, alog_ref, dskip_ref,
                     nw_ref, e_ref, ys_ref, conv_out_ref, ssm_out_ref):
    xn = xbc_ref[0]
    acc = cb_ref[...] + xn * cw_ref[CONV_W - 1:CONV_W, :]
    for jj in range(CONV_W - 1):
        acc = acc + conv_ref[0, jj:jj + 1, :] * cw_ref[jj:jj + 1, :]
        if jj > 0:
            conv_out_ref[0, jj - 1:jj, :] = conv_ref[0, jj:jj + 1, :]
    u = _silu(acc)
    conv_out_ref[0, CONV_W - 2:CONV_W - 1, :] = xn

    xs = u[:, :D_INNER]
    e01 = e_ref[...]
    dt = _softplus(dt_ref[0] + dtb_ref[...])
    d_a = jnp.exp(dt * (-jnp.exp(alog_ref[...])))
    rows8 = lax.broadcasted_iota(jnp.int32, (SUBLANES, DT_PAD), 0)
    two = jnp.where(rows8 == 0, dt, jnp.where(rows8 == 1, d_a, 0.0))
    two_exp = _exact_right(two, e01)
    xdt = xs * two_exp[0:1, :]
    rows = lax.broadcasted_iota(jnp.int32, (LANES, D_INNER), 0)
    stack = jnp.where(rows == 0, xdt, jnp.where(rows == 1, two_exp[1:2, :], 0.0))
    rows_c = lax.broadcasted_iota(jnp.int32, (SUBLANES, D_STATE), 0)
    c8 = jnp.zeros((SUBLANES, D_STATE), F32)
    for g in range(N_GROUPS):
        c_g = u[:, D_INNER + (N_GROUPS + g) * D_STATE:D_INNER + (N_GROUPS + g + 1) * D_STATE]
        c8 = jnp.where(rows_c == g, c_g, c8)
    c8 = c8.astype(BF16)
    blocks_per_group = D_INNER // N_GROUPS // LANES
    y_blocks = []
    for cbk in range(D_INNER // LANES):
        g = cbk // blocks_per_group
        cols = stack[:, cbk * LANES:(cbk + 1) * LANES].T
        b_row = u[:, D_INNER + g * D_STATE:D_INNER + (g + 1) * D_STATE]
        s_old = ssm_ref[0, cbk * LANES:(cbk + 1) * LANES, :]
        s_new = cols[:, 1:2] * s_old + cols[:, 0:1] * b_row
        ssm_out_ref[0, cbk * LANES:(cbk + 1) * LANES, :] = s_new
        yg = _dot_nt(c8, s_new.astype(BF16))
        y_blocks.append(yg[g:g + 1, :])
    y = jnp.concatenate(y_blocks, axis=1) + dskip_ref[...] * xs
    gw = D_INNER // N_GROUPS
    for g, o in enumerate(_gate_and_group_norm(y, z_ref[0], nw_ref[...])):
        ys_ref[0, :, g * gw:(g + 1) * gw] = o.astype(ys_ref.dtype)


def _ssd_step(xbc, z, dt, state_conv, state_ssm, conv_w, conv_b, dtb_pad, alog_pad, dskip_exp, nw, e01):
    nseq = xbc.shape[0]
    full = lambda shape: pl.BlockSpec(shape, lambda s: tuple(0 for _ in shape))
    per = lambda shape: pl.BlockSpec((1,) + shape, lambda s: (s, 0, 0))
    return pl.pallas_call(
        _ssd_step_kernel,
        grid=(nseq,),
        in_specs=[per((1, CONV_DIM)), per((1, D_INNER)), per((1, DT_PAD)), per((CONV_W - 1, CONV_DIM)),
                  per((D_INNER, D_STATE)),
                  full((CONV_W, CONV_DIM)), full((1, CONV_DIM)), full((1, DT_PAD)), full((1, DT_PAD)),
                  full((1, D_INNER)), full((1, D_INNER)), full((LANES, D_INNER))],
        out_specs=[per((1, D_INNER)), per((CONV_W - 1, CONV_DIM)), per((D_INNER, D_STATE))],
        out_shape=[jax.ShapeDtypeStruct((nseq, 1, D_INNER), BF16),
                   jax.ShapeDtypeStruct((nseq, CONV_W - 1, CONV_DIM), F32),
                   jax.ShapeDtypeStruct((nseq, D_INNER, D_STATE), F32)],
        compiler_params=pltpu.CompilerParams(dimension_semantics=("arbitrary",), vmem_limit_bytes=VMEM_LIMIT),
        name="ssd_step",
    )(xbc, z, dt, state_conv, state_ssm, conv_w, conv_b, dtb_pad, alog_pad, dskip_exp, nw, e01)


def _merge_kernel(x_ref, o_ref, ys_ref, g_ref, g1_ref, wa_ref, wb_ref, wo_ref, x1_ref):
    ya = _dot(o_ref[...], wa_ref[...])
    yb = _dot(ys_ref[...], wb_ref[...])
    gates = _sigmoid(g_ref[...])
    mix = gates[:, :D_MODEL] * ya + gates[:, D_MODEL:] * yb
    mixed = _dot(mix.astype(BF16), wo_ref[...])
    x1_ref[...] = x_ref[...] + g1_ref[0] * mixed


def _merge(x2d, o2d, ys2d, gate2d, g1, wa, wb, wo, tm, rows_per_mod):
    rows = x2d.shape[0]
    mod_rows = g1.shape[1]
    tiles_per_mod = rows_per_mod // tm
    row_blk = lambda width: pl.BlockSpec((tm, width), lambda i: (i, 0))
    wfull = lambda shape: pl.BlockSpec(shape, lambda i: (0, 0), pipeline_mode=pl.Buffered(1))
    return pl.pallas_call(
        _merge_kernel,
        grid=(rows // tm,),
        in_specs=[row_blk(D_MODEL), row_blk(ATTN_WIDTH), row_blk(D_INNER), row_blk(GATE_WIDTH),
                  pl.BlockSpec((1, mod_rows, D_MODEL), lambda i: (i // tiles_per_mod, 0, 0)),
                  wfull((ATTN_WIDTH, D_MODEL)), wfull((D_INNER, D_MODEL)), wfull((D_MODEL, D_MODEL))],
        out_specs=row_blk(D_MODEL),
        out_shape=jax.ShapeDtypeStruct((rows, D_MODEL), F32),
        compiler_params=pltpu.CompilerParams(dimension_semantics=("arbitrary",), vmem_limit_bytes=VMEM_LIMIT),
        name="merge",
    )(x2d, o2d, ys2d, gate2d, g1, wa, wb, wo)


def _ffn_kernel(x_ref, sh_ref, sc_ref, g2_ref, nw_ref, fw_ref, wg_ref, wu_ref, wd_ref, y_ref):
    x = x_ref[...]
    ms = jnp.mean(x * x, axis=-1, keepdims=True)
    h = x * lax.rsqrt(ms + EPS) * nw_ref[...]
    hb = (h * (1.0 + sc_ref[0]) + sh_ref[0]).astype(BF16)
    ff = _silu(_dot(hb, wg_ref[...])) * _dot(hb, wu_ref[...])
    x2 = x + g2_ref[0] * _dot(ff.astype(BF16), wd_ref[...])
    ms2 = jnp.mean(x2 * x2, axis=-1, keepdims=True)
    y_ref[...] = x2 * lax.rsqrt(ms2 + EPS) * fw_ref[...]


def _ffn(x2d, sh, sc, g2, nw, fw, wg, wu, wd, tm, rows_per_mod):
    rows = x2d.shape[0]
    mod_rows = sh.shape[1]
    tiles_per_mod = rows_per_mod // tm
    row_blk = pl.BlockSpec((tm, D_MODEL), lambda i: (i, 0))
    mod_blk = pl.BlockSpec((1, mod_rows, D_MODEL), lambda i: (i // tiles_per_mod, 0, 0))
    vec = pl.BlockSpec((1, D_MODEL), lambda i: (0, 0))
    wfull = lambda shape: pl.BlockSpec(shape, lambda i: (0, 0), pipeline_mode=pl.Buffered(1))
    return pl.pallas_call(
        _ffn_kernel,
        grid=(rows // tm,),
        in_specs=[row_blk, mod_blk, mod_blk, mod_blk, vec, vec,
                  wfull((D_MODEL, D_FF)), wfull((D_MODEL, D_FF)), wfull((D_FF, D_MODEL))],
        out_specs=row_blk,
        out_shape=jax.ShapeDtypeStruct((rows, D_MODEL), F32),
        compiler_params=pltpu.CompilerParams(dimension_semantics=("arbitrary",), vmem_limit_bytes=VMEM_LIMIT),
        name="ffn",
    )(x2d, sh, sc, g2, nw, fw, wg, wu, wd)


def _layer_in(x2d, mods, pos, p, tm, rows_per_mod, with_vt):
    sh1, sc1 = mods[0], mods[1]
    ct, s1, s2 = _rope_tables(pos)
    return _in_proj(x2d, sh1, sc1, p['norm1_w'], p['w_in'], ct, s1, s2, tm, rows_per_mod, with_vt)


def _layer_out(x2d, mods, o, ys, gate, p, tm, rows_per_mod):
    _, _, g1, sh2, sc2, g2 = mods
    x1 = _merge(x2d, o, ys, gate, g1, p['w_branch_attn'], p['w_branch_ssm'], p['w_out'], tm, rows_per_mod)
    return _ffn(x1, sh2, sc2, g2, p['norm2_w'], p['final_norm_w'], p['w_ffn_gate'], p['w_ffn_up'],
                p['w_ffn_down'], tm, rows_per_mod)


def kernel(x_prompt, x_sample, c_prompt, c_sample, cache_k, cache_v, page_table, state_conv, state_ssm, w_ada, b_ada, norm1_w, w_in, lambda_q1, lambda_k1, lambda_q2, lambda_k2, subln_w, conv_w, conv_b, dt_bias, a_log, d_skip, ssm_norm_w, w_branch_attn, w_branch_ssm, w_out, norm2_w, w_ffn_gate, w_ffn_up, w_ffn_down, final_norm_w):
    depth = w_in.shape[0]
    assert depth == 1
    bp, L, d = x_prompt.shape
    ns, Ld, _ = x_sample.shape
    assert Ld == 1 and d == D_MODEL
    n_pages = page_table.shape[1]
    past = n_pages * PAGE_SIZE
    lam_init = 0.8 - 0.6 * math.exp(-0.3 * 0)
    i = 0

    splits = np.cumsum([QK_WIDTH, QK_WIDTH, ATTN_WIDTH, D_INNER, CONV_DIM, SSM_HEADS])
    wq, wk, wv, wz, wx, wdt, wg = jnp.split(w_in[i], splits, axis=1)
    w_in_r = jnp.concatenate(
        [wq, wk, wv, wz, wx, wg, wdt, jnp.zeros((D_MODEL, DT_PAD - SSM_HEADS), F32)], axis=1).astype(BF16)
    pad_heads = lambda t: jnp.concatenate([t, jnp.zeros((DT_PAD - SSM_HEADS,), F32)]).reshape(1, DT_PAD)
    e01 = (jnp.arange(LANES)[:, None] == (jnp.arange(D_INNER)[None, :] // SSM_HEAD_DIM)).astype(BF16)
    p = {
        'norm1_w': norm1_w[i].reshape(1, -1), 'w_in': w_in_r,
        'w_branch_attn': w_branch_attn[i].astype(BF16), 'w_branch_ssm': w_branch_ssm[i].astype(BF16),
        'w_out': w_out[i].astype(BF16), 'norm2_w': norm2_w[i].reshape(1, -1),
        'final_norm_w': final_norm_w.reshape(1, -1),
        'w_ffn_gate': w_ffn_gate[i].astype(BF16), 'w_ffn_up': w_ffn_up[i].astype(BF16),
        'w_ffn_down': w_ffn_down[i].astype(BF16),
    }
    lam_vecs = [t[i].reshape(1, HEAD_DIM) for t in (lambda_q1, lambda_k1, lambda_q2, lambda_k2)]
    sw = subln_w[i].reshape(1, V_DIM)
    mamba_params = (conv_w[i], conv_b[i].reshape(1, -1), pad_heads(dt_bias[i]), pad_heads(a_log[i]),
                    jnp.repeat(d_skip[i], SSM_HEAD_DIM).reshape(1, D_INNER), ssm_norm_w[i].reshape(1, -1), e01)

    n_c = bp + ns
    c_rows = -(-n_c // SUBLANES) * SUBLANES
    c_all = jnp.concatenate([c_prompt, c_sample, jnp.zeros((c_rows - n_c, d), F32)], axis=0)
    mod = _ada_mod(c_all, w_ada[i], b_ada[i])
    mods_p = [m.reshape(bp, 1, d) for m in jnp.split(mod[:bp], 6, axis=1)]
    mods_s = [m.reshape(1, ns, d) for m in jnp.split(mod[bp:bp + ns], 6, axis=1)]

    xp2d = x_prompt.reshape(bp * L, d)
    xs2d = x_sample.reshape(ns, d)
    qb_p, kp, kb_p, vp, z_p, xbc_p, gate_p, dt_p, vt_p = _layer_in(
        xp2d, mods_p, jnp.arange(L), p, tm=256, rows_per_mod=L, with_vt=True)
    qb_s, ks, _, vs, z_s, xbc_s, gate_s, dt_s = _layer_in(
        xs2d, mods_s, jnp.full((ns,), past), p, tm=ns, rows_per_mod=ns, with_vt=False)

    pool = cache_k.shape[1]
    ck = cache_k[i].reshape(pool, PAGE_SIZE * N_HEADS, V_DIM)
    cv = cache_v[i].reshape(pool, PAGE_SIZE * N_HEADS, V_DIM)
    qh = qb_s.reshape(ns, N_HEADS, V_DIM)
    lane = jnp.arange(V_DIM)[None, None, :]
    q16 = jnp.concatenate([jnp.where(lane < HEAD_DIM, qh, 0), jnp.where(lane >= HEAD_DIM, qh, 0)], axis=1)
    padn = lambda t: jnp.concatenate(
        [t.reshape(ns, N_HEADS, V_DIM), jnp.zeros((ns, LANES - N_HEADS, V_DIM), F32)], axis=1)
    r3 = lambda t: t.reshape(bp, L, -1)
    o_p, o_s = _attention(r3(qb_p), r3(kb_p), vt_p, q16, padn(ks), padn(vs), ck, cv, page_table,
                          *lam_vecs, sw, lam_init, tq=512, pages=16, group=4)

    chunk = 128
    tri = jnp.tril(jnp.ones((chunk, chunk), F32)).astype(BF16)
    xbc3 = xbc_p.reshape(bp, L, CONV_DIM)
    ys_p, sp = _ssd_prompt(xbc3, z_p.reshape(bp, L, D_INNER), dt_p.reshape(bp, L, DT_PAD),
                           *mamba_params[:6], e01, tri, chunk)
    cp = xbc3[:, L - (CONV_W - 1):, :]
    ys_s, cs, ss = _ssd_step(xbc_s.reshape(ns, 1, CONV_DIM), z_s.reshape(ns, 1, D_INNER),
                             dt_s.reshape(ns, 1, DT_PAD), state_conv[i],
                             state_ssm[i].reshape(ns, D_INNER, D_STATE), *mamba_params)

    yp = _layer_out(xp2d, mods_p, o_p.reshape(bp * L, ATTN_WIDTH), ys_p.reshape(bp * L, D_INNER), gate_p, p,
                    tm=512, rows_per_mod=L)
    ys_ = _layer_out(xs2d, mods_s, o_s.reshape(ns, ATTN_WIDTH), ys_s.reshape(ns, D_INNER), gate_s, p,
                     tm=ns, rows_per_mod=ns)

    hk = (N_HEADS, 2 * HEAD_DIM)
    return (yp.reshape(bp, L, d), ys_.reshape(ns, 1, d),
            kp.reshape((1, bp, L) + hk), vp.reshape((1, bp, L) + hk),
            cp.reshape(1, bp, CONV_W - 1, CONV_DIM), sp.reshape(1, bp, SSM_HEADS, SSM_HEAD_DIM, D_STATE),
            ks.reshape((1, ns, 1) + hk), vs.reshape((1, ns, 1) + hk),
            cs.reshape(1, ns, CONV_W - 1, CONV_DIM), ss.reshape(1, ns, SSM_HEADS, SSM_HEAD_DIM, D_STATE))
```

```python
import functools
import math

import jax
import jax.numpy as jnp
import numpy as np
from jax import lax
from jax.experimental import pallas as pl
from jax.experimental.pallas import tpu as pltpu

F32 = jnp.float32
BF16 = jnp.bfloat16

D_MODEL = 1024
N_HEADS = 8
HEAD_DIM = 64
V_DIM = 2 * HEAD_DIM
QK_WIDTH = N_HEADS * 2 * HEAD_DIM
ATTN_WIDTH = N_HEADS * V_DIM
ROPE_DIM = HEAD_DIM // 4
ROPE_THETA = 500000.0
D_INNER = 2 * D_MODEL
SSM_HEAD_DIM = 64
SSM_HEADS = D_INNER // SSM_HEAD_DIM
N_GROUPS = 4
HEADS_PER_GROUP = SSM_HEADS // N_GROUPS
D_STATE = 128
CONV_W = 4
CONV_DIM = D_INNER + 2 * N_GROUPS * D_STATE
D_FF = -(-8 * D_MODEL // (3 * 256)) * 256
GATE_WIDTH = 2 * D_MODEL
PAGE_SIZE = 128
EPS = 1e-6

LANES = 128
SUBLANES = 8
DT_PAD = LANES
IN_WIDTH_PAD = 2 * QK_WIDTH + ATTN_WIDTH + D_INNER + CONV_DIM + GATE_WIDTH + DT_PAD
VMEM_LIMIT = 56 * 1024 * 1024

_OQ = 0
_OK = _OQ + QK_WIDTH
_OV = _OK + QK_WIDTH
_OZ = _OV + ATTN_WIDTH
_OX = _OZ + D_INNER
_OG = _OX + CONV_DIM
_OD = _OG + GATE_WIDTH


def _silu(x):
    return x * (1.0 / (1.0 + jnp.exp(-x)))


def _sigmoid(x):
    return 1.0 / (1.0 + jnp.exp(-x))


def _softplus(x):
    return jnp.maximum(x, 0.0) + jnp.log(1.0 + jnp.exp(-jnp.abs(x)))


def _split3(a):
    hi = a.astype(BF16)
    r1 = a - hi.astype(F32)
    mid = r1.astype(BF16)
    lo = (r1 - mid.astype(F32)).astype(BF16)
    return hi, mid, lo


def _dot(a, b):
    return jnp.dot(a, b, preferred_element_type=F32)


def _dot_nt(a, b):
    return lax.dot_general(a, b, (((1,), (1,)), ((), ())), preferred_element_type=F32)


def _exact_right(a, e01):
    hi, mid, lo = _split3(a)
    return _dot(hi, e01) + _dot(mid, e01) + _dot(lo, e01)


def _exact_left(t01, a):
    hi, mid, lo = _split3(a)
    return _dot(t01, hi) + _dot(t01, mid) + _dot(t01, lo)


def _ada_kernel(c_ref, w_ref, b_ref, o_ref):
    c = _silu(c_ref[...]).astype(BF16)
    o_ref[...] = _dot(c, w_ref[...].astype(BF16)) + b_ref[...]


def _ada_mod(c_all, w_ada, b_ada):
    rows = c_all.shape[0]
    tn = D_MODEL
    return pl.pallas_call(
        _ada_kernel,
        grid=(6 * D_MODEL // tn,),
        in_specs=[pl.BlockSpec((rows, D_MODEL), lambda j: (0, 0)),
                  pl.BlockSpec((D_MODEL, tn), lambda j: (0, j)),
                  pl.BlockSpec((1, tn), lambda j: (0, j))],
        out_specs=pl.BlockSpec((rows, tn), lambda j: (0, j)),
        out_shape=jax.ShapeDtypeStruct((rows, 6 * D_MODEL), F32),
        name="ada_mod",
    )(c_all, w_ada, b_ada.reshape(1, -1))


def _rope_cols(t, c_tab, s1_tab, s2_tab, scale):
    outs = []
    for cb in range(t.shape[1] // LANES):
        tc = t[:, cb * LANES:(cb + 1) * LANES]
        up = pltpu.roll(tc, LANES - ROPE_DIM // 2, axis=1)
        dn = pltpu.roll(tc, ROPE_DIM // 2, axis=1)
        r = tc * c_tab + up * s1_tab + dn * s2_tab
        outs.append(r * scale if scale != 1.0 else r)
    return outs


def _inproj_kernel(x_ref, sh_ref, sc_ref, nw_ref, w_ref, ct_ref, s1_ref, s2_ref,
                   q_ref, k_ref, kb_ref, v_ref, z_ref, xbc_ref, g_ref, dt_ref, *maybe_vt_ref):
    x = x_ref[...]
    ms = jnp.mean(x * x, axis=-1, keepdims=True)
    h = x * lax.rsqrt(ms + EPS) * nw_ref[...]
    h = h * (1.0 + sc_ref[0]) + sh_ref[0]
    hb = h.astype(BF16)

    def proj(lo, width):
        return _dot(hb, w_ref[:, lo:lo + width])

    ct, s1, s2 = ct_ref[...], s1_ref[...], s2_ref[...]
    q = _rope_cols(proj(_OQ, QK_WIDTH), ct, s1, s2, math.log2(math.e) / math.sqrt(HEAD_DIM))
    for cb, r in enumerate(q):
        q_ref[:, cb * LANES:(cb + 1) * LANES] = r.astype(BF16)
    k = _rope_cols(proj(_OK, QK_WIDTH), ct, s1, s2, 1.0)
    for cb, r in enumerate(k):
        k_ref[:, cb * LANES:(cb + 1) * LANES] = r
        kb_ref[:, cb * LANES:(cb + 1) * LANES] = r.astype(BF16)
    v = proj(_OV, ATTN_WIDTH)
    v_ref[...] = v
    for vt_ref in maybe_vt_ref:
        for cb in range(ATTN_WIDTH // LANES):
            vt_ref[0, cb * LANES:(cb + 1) * LANES, :] = v[:, cb * LANES:(cb + 1) * LANES].T.astype(BF16)
    z_ref[...] = proj(_OZ, D_INNER)
    xbc_ref[...] = proj(_OX, CONV_DIM)
    g_ref[...] = proj(_OG, GATE_WIDTH)
    dt_ref[...] = proj(_OD, DT_PAD)


def _in_proj(x2d, sh, sc, nw, w_in, ctab, s1tab, s2tab, tm, rows_per_mod, with_vt):
    rows = x2d.shape[0]
    mod_rows = sh.shape[1]
    tiles_per_mod = rows_per_mod // tm
    row_blk = lambda width: pl.BlockSpec((tm, width), lambda i: (i, 0))
    mod_blk = pl.BlockSpec((1, mod_rows, D_MODEL), lambda i: (i // tiles_per_mod, 0, 0))
    assert ctab.shape[0] == rows_per_mod
    tab_blk = pl.BlockSpec((tm, LANES), lambda i: (i % tiles_per_mod, 0))
    outs = [(QK_WIDTH, BF16), (QK_WIDTH, F32), (QK_WIDTH, BF16), (ATTN_WIDTH, F32),
            (D_INNER, F32), (CONV_DIM, F32), (GATE_WIDTH, F32), (DT_PAD, F32)]
    out_specs = [row_blk(w) for w, _ in outs]
    out_shape = [jax.ShapeDtypeStruct((rows, w), dt) for w, dt in outs]
    if with_vt:
        out_specs.append(pl.BlockSpec((1, ATTN_WIDTH, tm), lambda i: (i, 0, 0)))
        out_shape.append(jax.ShapeDtypeStruct((rows // tm, ATTN_WIDTH, tm), BF16))
    return pl.pallas_call(
        _inproj_kernel,
        grid=(rows // tm,),
        in_specs=[row_blk(D_MODEL), mod_blk, mod_blk,
                  pl.BlockSpec((1, D_MODEL), lambda i: (0, 0)),
                  pl.BlockSpec((D_MODEL, IN_WIDTH_PAD), lambda i: (0, 0), pipeline_mode=pl.Buffered(1)),
                  tab_blk, tab_blk, tab_blk],
        out_specs=out_specs,
        out_shape=out_shape,
        compiler_params=pltpu.CompilerParams(dimension_semantics=("arbitrary",), vmem_limit_bytes=VMEM_LIMIT),
        name="in_proj",
    )(x2d, sh, sc, nw, w_in, ctab, s1tab, s2tab)


def _rope_tables(pos):
    half = ROPE_DIM // 2
    inv_freq = ROPE_THETA ** (-jnp.arange(half, dtype=F32) * 2.0 / ROPE_DIM)
    ang = pos.astype(F32)[:, None] * inv_freq[None, :]
    cos, sin = jnp.cos(ang), jnp.sin(ang)
    n = pos.shape[0]
    rest = HEAD_DIM - ROPE_DIM
    c64 = jnp.concatenate([cos, cos, jnp.ones((n, rest), F32)], axis=1)
    s1_64 = jnp.concatenate([-sin, jnp.zeros((n, half + rest), F32)], axis=1)
    s2_64 = jnp.concatenate([jnp.zeros((n, half), F32), sin, jnp.zeros((n, rest), F32)], axis=1)
    rep = LANES // HEAD_DIM
    return jnp.tile(c64, (1, rep)), jnp.tile(s1_64, (1, rep)), jnp.tile(s2_64, (1, rep))


def _lambda_value(lq1, lk1, lq2, lk2, lam_init):
    a = jnp.sum(lq1[...] * lk1[...], axis=1, keepdims=True)
    b = jnp.sum(lq2[...] * lk2[...], axis=1, keepdims=True)
    return jnp.exp(a) - jnp.exp(b) + lam_init


def _prompt_attn_part(qi, q_ref, k_ref, vt_ref, lam, sw_ref, o_ref,
                      qm_ref, sa_ref, sb_ref, m_ref, acc_ref, *, tq, tk, lam_init, between):
    assert tq == 2 * tk
    q = q_ref[0]
    lane = lax.broadcasted_iota(jnp.int32, q.shape, 1)
    zero = jnp.zeros_like(q)
    qm_ref[0] = jnp.where(lane < HEAD_DIM, q, zero)
    qm_ref[1] = jnp.where(lane >= HEAD_DIM, q, zero)
    m_ref[...] = jnp.full(m_ref.shape, -jnp.inf, F32)
    acc_ref[...] = jnp.zeros(acc_ref.shape, F32)

    def scores(ki, s_ref):
        k = k_ref[0, pl.ds(pl.multiple_of(ki * tk, tk), tk), :]
        for mp in range(2):
            s_ref[mp] = _dot_nt(k, qm_ref[mp])

    def absorb(ki, s_ref, diag):
        vt = jnp.concatenate([vt_ref[ki], jnp.ones((acc_ref.shape[1] - V_DIM, tk), BF16)], axis=0)
        for mp in range(2):
            st = s_ref[mp]
            if diag is not None:
                key = diag * tk + lax.broadcasted_iota(jnp.int32, st.shape, 0)
                qry = lax.broadcasted_iota(jnp.int32, st.shape, 1)
                st = jnp.where(key <= qry, st, -jnp.inf)
            m_prev = m_ref[mp]
            m_new = jnp.maximum(m_prev, jnp.max(st, axis=0, keepdims=True))
            alpha = jnp.exp2(m_prev - m_new)
            pt = jnp.exp2(st - m_new)
            acc_ref[mp] = alpha * acc_ref[mp] + _dot(vt, pt.astype(BF16))
            m_ref[mp] = m_new

    n_full = 2 * qi
    scores(0, sa_ref)

    def pair(t0):
        scores(t0 + 1, sb_ref)
        absorb(t0, sa_ref, None)
        scores(t0 + 2, sa_ref)
        absorb(t0 + 1, sb_ref, None)

    def body(j, carry):
        pair(4 * j)
        pair(4 * j + 2)
        return carry

    lax.fori_loop(0, lax.shift_right_logical(qi, 1), body, 0)

    @pl.when(lax.bitwise_and(qi, 1) == 1)
    def _():
        pair(n_full - 2)

    scores(n_full + 1, sb_ref)
    between()
    absorb(n_full, sa_ref, 0)
    absorb(n_full + 1, sb_ref, 1)

    o1 = acc_ref[0, 0:V_DIM, :] / acc_ref[0, V_DIM:V_DIM + 1, :]
    o2 = acc_ref[1, 0:V_DIM, :] / acc_ref[1, V_DIM:V_DIM + 1, :]
    ot = o1 - lam * o2
    ms = jnp.mean(ot * ot, axis=0, keepdims=True)
    ot = ot * lax.rsqrt(ms + EPS) * sw_ref[...] * (1.0 - lam_init)
    o_ref[0] = ot.T.astype(o_ref.dtype)


def _sample_attn_part(j, n_j, q_ref, kn_ref, vn_ref, k_refs, v_refs, lam, sw_ref, o_ref,
                      m_ref, l_ref, acc_ref, *, group, lam_init):
    pages = len(k_refs)
    nrow = 2 * N_HEADS
    q16 = q_ref[0]

    def head_mask(nkeys):
        row = lax.broadcasted_iota(jnp.int32, (nrow, nkeys), 0)
        col = lax.broadcasted_iota(jnp.int32, (nrow, nkeys), 1)
        return (row % N_HEADS) == (col % N_HEADS)

    def scores(ks, valid):
        return [jnp.where(valid, _dot(q16, k2d.astype(BF16).T), -jnp.inf) for k2d in ks]

    def absorb(ss, vs):
        m_blk = jnp.max(ss[0], axis=1, keepdims=True)
        for s in ss[1:]:
            m_blk = jnp.maximum(m_blk, jnp.max(s, axis=1, keepdims=True))
        m_prev = m_ref[...]
        m_new = jnp.maximum(m_prev, m_blk)
        alpha = jnp.exp2(m_prev - m_new)
        l_new = alpha * l_ref[...]
        acc = alpha * acc_ref[...]
        for s, v2d in zip(ss, vs):
            p = jnp.exp2(s - m_new)
            l_new = l_new + jnp.sum(p, axis=1, keepdims=True)
            acc = acc + _dot(p.astype(BF16), v2d)
        l_ref[...] = l_new
        acc_ref[...] = acc
        m_ref[...] = m_new

    def start():
        @pl.when(j == 0)
        def _():
            m_ref[...] = jnp.full(m_ref.shape, -jnp.inf, F32)
            l_ref[...] = jnp.zeros(l_ref.shape, F32)
            acc_ref[...] = jnp.zeros(acc_ref.shape, F32)
            col = lax.broadcasted_iota(jnp.int32, (nrow, LANES), 1)
            absorb(scores([kn_ref[0]], head_mask(LANES) & (col < N_HEADS)), [vn_ref[0].astype(BF16)])

    def absorb_pages():
        valid = head_mask(PAGE_SIZE * N_HEADS)
        bounds = list(range(0, pages + 1, group))
        ss = scores([r[0] for r in k_refs[0:group]], valid)
        for lo, hi in zip(bounds[:-1], bounds[1:]):
            nxt = scores([r[0] for r in k_refs[hi:hi + group]], valid) if hi < pages else None
            absorb(ss, [r[0].astype(BF16) for r in v_refs[lo:hi]])
            ss = nxt

    def finish():
        @pl.when(j == n_j - 1)
        def _():
            o1 = acc_ref[0:N_HEADS, :] / l_ref[0:N_HEADS, :]
            o2 = acc_ref[N_HEADS:nrow, :] / l_ref[N_HEADS:nrow, :]
            o = o1 - lam * o2
            ms = jnp.mean(o * o, axis=-1, keepdims=True)
            o_ref[0] = (o * lax.rsqrt(ms + EPS) * sw_ref[...] * (1.0 - lam_init)).astype(o_ref.dtype)

    return start, absorb_pages, finish


def _attn_kernel(pt_ref, q_ref, k_ref, vt_ref, qs_ref, kn_ref, vn_ref, *rest, pages, group, steps_per_seq,
                 tq, tk, lam_init):
    k_refs = rest[:pages]
    v_refs = rest[pages:2 * pages]
    (lq1, lk1, lq2, lk2, sw_col_ref, sw_row_ref, o_ref, os_ref,
     qm_ref, sa_ref, sb_ref, m_ref, acc_ref, ms_ref, ls_ref, accs_ref) = rest[2 * pages:]
    step = (pl.program_id(0) * pl.num_programs(1) + pl.program_id(1)) * pl.num_programs(2) + pl.program_id(2)
    lam = _lambda_value(lq1, lk1, lq2, lk2, lam_init)
    start, absorb_pages, finish = _sample_attn_part(
        lax.bitwise_and(step, steps_per_seq - 1), steps_per_seq, qs_ref, kn_ref, vn_ref, k_refs, v_refs, lam,
        sw_row_ref, os_ref, ms_ref, ls_ref, accs_ref, group=group, lam_init=lam_init)
    start()
    _prompt_attn_part(pl.program_id(2), q_ref, k_ref, vt_ref, lam, sw_col_ref, o_ref,
                      qm_ref, sa_ref, sb_ref, m_ref, acc_ref, tq=tq, tk=tk, lam_init=lam_init,
                      between=absorb_pages)
    finish()


def _attention(qb, kb, vt, q16, kn_pad, vn_pad, cache_k, cache_v, page_table, lq1, lk1, lq2, lk2, subln_w,
               lam_init, tq, pages, group):
    b, L, _ = qb.shape
    tk = vt.shape[2]
    nq = L // tq
    nseq, n_pages = page_table.shape
    assert n_pages % pages == 0 and pages % group == 0
    steps_per_seq = n_pages // pages
    assert b * N_HEADS * nq == nseq * steps_per_seq

    sps_log2 = steps_per_seq.bit_length() - 1
    assert steps_per_seq == 1 << sps_log2

    def seq_of(bi, h, qi):
        return lax.shift_right_logical((bi * N_HEADS + h) * nq + qi, sps_log2)

    def page_blk(i):
        def idx(bi, h, qi, pt):
            return (pt[((bi * N_HEADS + h) * nq + qi) * pages + i], 0, 0)
        return pl.BlockSpec((1, PAGE_SIZE * N_HEADS, V_DIM), idx)

    const = lambda shape: pl.BlockSpec(shape, lambda bi, h, qi, pt: (0, 0))
    per_seq = lambda rows: pl.BlockSpec((1, rows, V_DIM), lambda bi, h, qi, pt: (seq_of(bi, h, qi), 0, 0))
    q_blk = pl.BlockSpec((1, tq, V_DIM), lambda bi, h, qi, pt: (bi, qi, h))
    vec = const((1, HEAD_DIM))
    grid_spec = pltpu.PrefetchScalarGridSpec(
        num_scalar_prefetch=1,
        grid=(b, N_HEADS, nq),
        in_specs=[q_blk,
                  pl.BlockSpec((1, L, V_DIM), lambda bi, h, qi, pt: (bi, 0, h)),
                  pl.BlockSpec((L // tk, V_DIM, tk), lambda bi, h, qi, pt: (bi, h, 0)),
                  per_seq(2 * N_HEADS), per_seq(LANES), per_seq(LANES)]
                 + [page_blk(i) for i in range(pages)] + [page_blk(i) for i in range(pages)]
                 + [vec, vec, vec, vec, const((V_DIM, 1)), const((1, V_DIM))],
        out_specs=[q_blk, per_seq(N_HEADS)],
        scratch_shapes=[pltpu.VMEM((2, tq, V_DIM), BF16),
                        pltpu.VMEM((2, tk, tq), F32),
                        pltpu.VMEM((2, tk, tq), F32),
                        pltpu.VMEM((2, 1, tq), F32),
                        pltpu.VMEM((2, V_DIM + 16, tq), F32),
                        pltpu.VMEM((2 * N_HEADS, 1), F32),
                        pltpu.VMEM((2 * N_HEADS, 1), F32),
                        pltpu.VMEM((2 * N_HEADS, V_DIM), F32)],
    )
    return pl.pallas_call(
        functools.partial(_attn_kernel, pages=pages, group=group, steps_per_seq=steps_per_seq,
                          tq=tq, tk=tk, lam_init=lam_init),
        grid_spec=grid_spec,
        out_shape=[jax.ShapeDtypeStruct((b, L, ATTN_WIDTH), BF16),
                   jax.ShapeDtypeStruct((nseq, N_HEADS, V_DIM), BF16)],
        compiler_params=pltpu.CompilerParams(
            dimension_semantics=("arbitrary", "arbitrary", "arbitrary"), vmem_limit_bytes=VMEM_LIMIT),
        name="attn",
    )(page_table.reshape(-1), qb, kb, vt, q16, kn_pad, vn_pad, *([cache_k] * pages), *([cache_v] * pages),
      lq1, lk1, lq2, lk2, subln_w.reshape(V_DIM, 1), subln_w.reshape(1, V_DIM))


def _gate_and_group_norm(y, z, nw):
    y = y * _silu(z)
    gw = D_INNER // N_GROUPS
    outs = []
    for g in range(N_GROUPS):
        yg = y[:, g * gw:(g + 1) * gw]
        ms = jnp.mean(yg * yg, axis=-1, keepdims=True)
        outs.append(yg * lax.rsqrt(ms + EPS) * nw[:, g * gw:(g + 1) * gw])
    return outs


def _ssd_prompt_kernel(xbc_ref, z_ref, dt_ref, cw_ref, cb_ref, dtb_ref, alog_ref, dskip_ref, nw_ref,
                       e_ref, tri_ref, ys_ref, ssm_ref, ext_ref, st_ref, y_ref, *, chunk):
    c = pl.program_id(1)
    halo = SUBLANES

    @pl.when(c == 0)
    def _():
        ext_ref[0:halo, :] = jnp.zeros((halo, CONV_DIM), F32)
        st_ref[...] = jnp.zeros(st_ref.shape, F32)

    xt = xbc_ref[0]
    ext_ref[halo:halo + chunk, :] = xt
    acc = cb_ref[...] + ext_ref[halo - 3:halo - 3 + chunk, :] * cw_ref[0:1, :]
    for jj in range(1, CONV_W):
        acc = acc + ext_ref[halo - 3 + jj:halo - 3 + jj + chunk, :] * cw_ref[jj:jj + 1, :]
    u = _silu(acc)
    ext_ref[0:halo, :] = xt[chunk - halo:chunk, :]

    xs = u[:, :D_INNER]
    tri = tri_ref[...]
    e01 = e_ref[...]
    dt = _softplus(dt_ref[0] + dtb_ref[...])
    a = dt * (-jnp.exp(alog_ref[...]))
    a_cum = _exact_left(tri, a)
    a_cum_t = a_cum.T
    a_exp = _exact_right(a_cum, e01)
    dt_exp = _exact_right(dt, e01)
    ea = jnp.exp(a_exp)
    a_last = a_exp[chunk - 1:chunk, :]
    xdt = xs * dt_exp
    xw = (xdt * jnp.exp(a_last - a_exp)).astype(BF16)
    xdt_b = xdt.astype(BF16)
    causal = lax.broadcasted_iota(jnp.int32, (chunk, chunk), 0) >= lax.broadcasted_iota(jnp.int32, (chunk, chunk), 1)
    lane = lax.broadcasted_iota(jnp.int32, (chunk, LANES), 1)
    gw = D_INNER // N_GROUPS

    for g in range(N_GROUPS):
        bg = u[:, D_INNER + g * D_STATE:D_INNER + (g + 1) * D_STATE]
        cg = u[:, D_INNER + (N_GROUPS + g) * D_STATE:D_INNER + (N_GROUPS + g + 1) * D_STATE]
        bg_b, cg_b = bg.astype(BF16), cg.astype(BF16)
        cb = _dot_nt(cg_b, bg_b)
        st_g = st_ref[:, g * gw:(g + 1) * gw]
        y_off = _dot(cg_b, st_g.astype(BF16))
        s_new = _dot(bg.T.astype(BF16), xw[:, g * gw:(g + 1) * gw])
        st_ref[:, g * gw:(g + 1) * gw] = st_g * ea[chunk - 1:chunk, g * gw:(g + 1) * gw] + s_new
        for pp in range(HEADS_PER_GROUP // 2):
            col0 = g * gw + pp * LANES
            xpair = xdt_b[:, col0:col0 + LANES]
            ys_pair = []
            for hh in range(2):
                r = g * HEADS_PER_GROUP + 2 * pp + hh
                seg = a_cum[:, r:r + 1] - a_cum_t[r:r + 1, :]
                decay = jnp.exp(jnp.where(causal, seg, -jnp.inf))
                ys_pair.append(_dot((cb * decay).astype(BF16), xpair))
            y_diag = jnp.where(lane < SSM_HEAD_DIM, ys_pair[0], ys_pair[1])
            y_ref[:, col0:col0 + LANES] = y_diag + ea[:, col0:col0 + LANES] * y_off[:, pp * LANES:(pp + 1) * LANES]

    y = y_ref[...] + dskip_ref[...] * xs
    for g, o in enumerate(_gate_and_group_norm(y, z_ref[0], nw_ref[...])):
        ys_ref[0, :, g * gw:(g + 1) * gw] = o.astype(ys_ref.dtype)

    @pl.when(c == pl.num_programs(1) - 1)
    def _():
        ssm_ref[0] = st_ref[...].T


def _ssd_prompt(xbc, z, dt, conv_w, conv_b, dtb_pad, alog_pad, dskip_exp, nw, e01, tri, chunk):
    b, L, _ = xbc.shape
    full = lambda shape: pl.BlockSpec(shape, lambda bi, c: tuple(0 for _ in shape))
    return pl.pallas_call(
        functools.partial(_ssd_prompt_kernel, chunk=chunk),
        grid=(b, L // chunk),
        in_specs=[pl.BlockSpec((1, chunk, CONV_DIM), lambda bi, c: (bi, c, 0)),
                  pl.BlockSpec((1, chunk, D_INNER), lambda bi, c: (bi, c, 0)),
                  pl.BlockSpec((1, chunk, DT_PAD), lambda bi, c: (bi, c, 0)),
                  full((CONV_W, CONV_DIM)), full((1, CONV_DIM)), full((1, DT_PAD)), full((1, DT_PAD)),
                  full((1, D_INNER)), full((1, D_INNER)), full((LANES, D_INNER)), full((chunk, chunk))],
        out_specs=[pl.BlockSpec((1, chunk, D_INNER), lambda bi, c: (bi, c, 0)),
                   pl.BlockSpec((1, D_INNER, D_STATE), lambda bi, c: (bi, 0, 0))],
        out_shape=[jax.ShapeDtypeStruct((b, L, D_INNER), BF16),
                   jax.ShapeDtypeStruct((b, D_INNER, D_STATE), F32)],
        scratch_shapes=[pltpu.VMEM((SUBLANES + chunk, CONV_DIM), F32),
                        pltpu.VMEM((D_STATE, D_INNER), F32),
                        pltpu.VMEM((chunk, D_INNER), F32)],
        compiler_params=pltpu.CompilerParams(
            dimension_semantics=("arbitrary", "arbitrary"), vmem_limit_bytes=VMEM_LIMIT),
        name="ssd_prompt",
    )(xbc, z, dt, conv_w, conv_b, dtb_pad, alog_pad, dskip_exp, nw, e01, tri)


def _ssd_step_kernel(xbc_ref, z_ref, dt_ref, conv_ref, ssm_ref, cw_ref, cb_ref, dtb_ref, alog_ref, dskip_ref,
                     nw_ref, e_ref, ys_ref, conv_out_ref, ssm_out_ref):
    xn = xbc_ref[0]
    acc = cb_ref[...] + xn * cw_ref[CONV_W - 1:CONV_W, :]
    for jj in range(CONV_W - 1):
        acc = acc + conv_ref[0, jj:jj + 1, :] * cw_ref[jj:jj + 1, :]
        if jj > 0:
            conv_out_ref[0, jj - 1:jj, :] = conv_ref[0, jj:jj + 1, :]
    u = _silu(acc)
    conv_out_ref[0, CONV_W - 2:CONV_W - 1, :] = xn

    xs = u[:, :D_INNER]
    e01 = e_ref[...]
    dt = _softplus(dt_ref[0] + dtb_ref[...])
    d_a = jnp.exp(dt * (-jnp.exp(alog_ref[...])))
    rows8 = lax.broadcasted_iota(jnp.int32, (SUBLANES, DT_PAD), 0)
    two = jnp.where(rows8 == 0, dt, jnp.where(rows8 == 1, d_a, 0.0))
    two_exp = _exact_right(two, e01)
    xdt = xs * two_exp[0:1, :]
    rows = lax.broadcasted_iota(jnp.int32, (LANES, D_INNER), 0)
    stack = jnp.where(rows == 0, xdt, jnp.where(rows == 1, two_exp[1:2, :], 0.0))
    rows_c = lax.broadcasted_iota(jnp.int32, (SUBLANES, D_STATE), 0)
    c8 = jnp.zeros((SUBLANES, D_STATE), F32)
    for g in range(N_GROUPS):
        c_g = u[:, D_INNER + (N_GROUPS + g) * D_STATE:D_INNER + (N_GROUPS + g + 1) * D_STATE]
        c8 = jnp.where(rows_c == g, c_g, c8)
    c8 = c8.astype(BF16)
    blocks_per_group = D_INNER // N_GROUPS // LANES
    y_blocks = []
    for cbk in range(D_INNER // LANES):
        g = cbk // blocks_per_group
        cols = stack[:, cbk * LANES:(cbk + 1) * LANES].T
        b_row = u[:, D_INNER + g * D_STATE:D_INNER + (g + 1) * D_STATE]
        s_old = ssm_ref[0, cbk * LANES:(cbk + 1) * LANES, :]
        s_new = cols[:, 1:2] * s_old + cols[:, 0:1] * b_row
        ssm_out_ref[0, cbk * LANES:(cbk + 1) * LANES, :] = s_new
        yg = _dot_nt(c8, s_new.astype(BF16))
        y_blocks.append(yg[g:g + 1, :])
    y = jnp.concatenate(y_blocks, axis=1) + dskip_ref[...] * xs
    gw = D_INNER // N_GROUPS
    for g, o in enumerate(_gate_and_group_norm(y, z_ref[0], nw_ref[...])):
        ys_ref[0, :, g * gw:(g + 1) * gw] = o.astype(ys_ref.dtype)


def _ssd_step(xbc, z, dt, state_conv, state_ssm, conv_w, conv_b, dtb_pad, alog_pad, dskip_exp, nw, e01):
    nseq = xbc.shape[0]
    full = lambda shape: pl.BlockSpec(shape, lambda s: tuple(0 for _ in shape))
    per = lambda shape: pl.BlockSpec((1,) + shape, lambda s: (s, 0, 0))
    return pl.pallas_call(
        _ssd_step_kernel,
        grid=(nseq,),
        in_specs=[per((1, CONV_DIM)), per((1, D_INNER)), per((1, DT_PAD)), per((CONV_W - 1, CONV_DIM)),
                  per((D_INNER, D_STATE)),
                  full((CONV_W, CONV_DIM)), full((1, CONV_DIM)), full((1, DT_PAD)), full((1, DT_PAD)),
                  full((1, D_INNER)), full((1, D_INNER)), full((LANES, D_INNER))],
        out_specs=[per((1, D_INNER)), per((CONV_W - 1, CONV_DIM)), per((D_INNER, D_STATE))],
        out_shape=[jax.ShapeDtypeStruct((nseq, 1, D_INNER), BF16),
                   jax.ShapeDtypeStruct((nseq, CONV_W - 1, CONV_DIM), F32),
                   jax.ShapeDtypeStruct((nseq, D_INNER, D_STATE), F32)],
        compiler_params=pltpu.CompilerParams(dimension_semantics=("arbitrary",), vmem_limit_bytes=VMEM_LIMIT),
        name="ssd_step",
    )(xbc, z, dt, state_conv, state_ssm, conv_w, conv_b, dtb_pad, alog_pad, dskip_exp, nw, e01)


def _merge_kernel(x_ref, o_ref, ys_ref, g_ref, g1_ref, wa_ref, wb_ref, wo_ref, x1_ref):
    ya = _dot(o_ref[...], wa_ref[...])
    yb = _dot(ys_ref[...], wb_ref[...])
    gates = _sigmoid(g_ref[...])
    mix = gates[:, :D_MODEL] * ya + gates[:, D_MODEL:] * yb
    mixed = _dot(mix.astype(BF16), wo_ref[...])
    x1_ref[...] = x_ref[...] + g1_ref[0] * mixed


def _merge(x2d, o2d, ys2d, gate2d, g1, wa, wb, wo, tm, rows_per_mod):
    rows = x2d.shape[0]
    mod_rows = g1.shape[1]
    tiles_per_mod = rows_per_mod // tm
    row_blk = lambda width: pl.BlockSpec((tm, width), lambda i: (i, 0))
    wfull = lambda shape: pl.BlockSpec(shape, lambda i: (0, 0), pipeline_mode=pl.Buffered(1))
    return pl.pallas_call(
        _merge_kernel,
        grid=(rows // tm,),
        in_specs=[row_blk(D_MODEL), row_blk(ATTN_WIDTH), row_blk(D_INNER), row_blk(GATE_WIDTH),
                  pl.BlockSpec((1, mod_rows, D_MODEL), lambda i: (i // tiles_per_mod, 0, 0)),
                  wfull((ATTN_WIDTH, D_MODEL)), wfull((D_INNER, D_MODEL)), wfull((D_MODEL, D_MODEL))],
        out_specs=row_blk(D_MODEL),
        out_shape=jax.ShapeDtypeStruct((rows, D_MODEL), F32),
        compiler_params=pltpu.CompilerParams(dimension_semantics=("arbitrary",), vmem_limit_bytes=VMEM_LIMIT),
        name="merge",
    )(x2d, o2d, ys2d, gate2d, g1, wa, wb, wo)


def _ffn_kernel(x_ref, sh_ref, sc_ref, g2_ref, nw_ref, fw_ref, wg_ref, wu_ref, wd_ref, y_ref):
    x = x_ref[...]
    ms = jnp.mean(x * x, axis=-1, keepdims=True)
    h = x * lax.rsqrt(ms + EPS) * nw_ref[...]
    hb = (h * (1.0 + sc_ref[0]) + sh_ref[0]).astype(BF16)
    ff = _silu(_dot(hb, wg_ref[...])) * _dot(hb, wu_ref[...])
    x2 = x + g2_ref[0] * _dot(ff.astype(BF16), wd_ref[...])
    ms2 = jnp.mean(x2 * x2, axis=-1, keepdims=True)
    y_ref[...] = x2 * lax.rsqrt(ms2 + EPS) * fw_ref[...]


def _ffn(x2d, sh, sc, g2, nw, fw, wg, wu, wd, tm, rows_per_mod):
    rows = x2d.shape[0]
    mod_rows = sh.shape[1]
    tiles_per_mod = rows_per_mod // tm
    row_blk = pl.BlockSpec((tm, D_MODEL), lambda i: (i, 0))
    mod_blk = pl.BlockSpec((1, mod_rows, D_MODEL), lambda i: (i // tiles_per_mod, 0, 0))
    vec = pl.BlockSpec((1, D_MODEL), lambda i: (0, 0))
    wfull = lambda shape: pl.BlockSpec(shape, lambda i: (0, 0), pipeline_mode=pl.Buffered(1))
    return pl.pallas_call(
        _ffn_kernel,
        grid=(rows // tm,),
        in_specs=[row_blk, mod_blk, mod_blk, mod_blk, vec, vec,
                  wfull((D_MODEL, D_FF)), wfull((D_MODEL, D_FF)), wfull((D_FF, D_MODEL))],
        out_specs=row_blk,
        out_shape=jax.ShapeDtypeStruct((rows, D_MODEL), F32),
        compiler_params=pltpu.CompilerParams(dimension_semantics=("arbitrary",), vmem_limit_bytes=VMEM_LIMIT),
        name="ffn",
    )(x2d, sh, sc, g2, nw, fw, wg, wu, wd)


def _layer_in(x2d, mods, pos, p, tm, rows_per_mod, with_vt):
    sh1, sc1 = mods[0], mods[1]
    ct, s1, s2 = _rope_tables(pos)
    return _in_proj(x2d, sh1, sc1, p['norm1_w'], p['w_in'], ct, s1, s2, tm, rows_per_mod, with_vt)


def _layer_out(x2d, mods, o, ys, gate, p, tm, rows_per_mod):
    _, _, g1, sh2, sc2, g2 = mods
    x1 = _merge(x2d, o, ys, gate, g1, p['w_branch_attn'], p['w_branch_ssm'], p['w_out'], tm, rows_per_mod)
    return _ffn(x1, sh2, sc2, g2, p['norm2_w'], p['final_norm_w'], p['w_ffn_gate'], p['w_ffn_up'],
                p['w_ffn_down'], tm, rows_per_mod)


def kernel(x_prompt, x_sample, c_prompt, c_sample, cache_k, cache_v, page_table, state_conv, state_ssm, w_ada, b_ada, norm1_w, w_in, lambda_q1, lambda_k1, lambda_q2, lambda_k2, subln_w, conv_w, conv_b, dt_bias, a_log, d_skip, ssm_norm_w, w_branch_attn, w_branch_ssm, w_out, norm2_w, w_ffn_gate, w_ffn_up, w_ffn_down, final_norm_w):
    depth = w_in.shape[0]
    assert depth == 1
    bp, L, d = x_prompt.shape
    ns, Ld, _ = x_sample.shape
    assert Ld == 1 and d == D_MODEL
    n_pages = page_table.shape[1]
    past = n_pages * PAGE_SIZE
    lam_init = 0.8 - 0.6 * math.exp(-0.3 * 0)
    i = 0

    splits = np.cumsum([QK_WIDTH, QK_WIDTH, ATTN_WIDTH, D_INNER, CONV_DIM, SSM_HEADS])
    wq, wk, wv, wz, wx, wdt, wg = jnp.split(w_in[i].astype(BF16), splits, axis=1)
    w_in_r = jnp.concatenate(
        [wq, wk, wv, wz, wx, wg, wdt, jnp.zeros((D_MODEL, DT_PAD - SSM_HEADS), BF16)], axis=1)
    pad_heads = lambda t: jnp.concatenate([t, jnp.zeros((DT_PAD - SSM_HEADS,), F32)]).reshape(1, DT_PAD)
    e01 = (jnp.arange(LANES)[:, None] == (jnp.arange(D_INNER)[None, :] // SSM_HEAD_DIM)).astype(BF16)
    p = {
        'norm1_w': norm1_w[i].reshape(1, -1), 'w_in': w_in_r,
        'w_branch_attn': w_branch_attn[i].astype(BF16), 'w_branch_ssm': w_branch_ssm[i].astype(BF16),
        'w_out': w_out[i].astype(BF16), 'norm2_w': norm2_w[i].reshape(1, -1),
        'final_norm_w': final_norm_w.reshape(1, -1),
        'w_ffn_gate': w_ffn_gate[i].astype(BF16), 'w_ffn_up': w_ffn_up[i].astype(BF16),
        'w_ffn_down': w_ffn_down[i].astype(BF16),
    }
    lam_vecs = [t[i].reshape(1, HEAD_DIM) for t in (lambda_q1, lambda_k1, lambda_q2, lambda_k2)]
    sw = subln_w[i].reshape(1, V_DIM)
    mamba_params = (conv_w[i], conv_b[i].reshape(1, -1), pad_heads(dt_bias[i]), pad_heads(a_log[i]),
                    jnp.repeat(d_skip[i], SSM_HEAD_DIM).reshape(1, D_INNER), ssm_norm_w[i].reshape(1, -1), e01)

    n_c = bp + ns
    c_rows = -(-n_c // SUBLANES) * SUBLANES
    c_all = jnp.concatenate([c_prompt, c_sample, jnp.zeros((c_rows - n_c, d), F32)], axis=0)
    mod = _ada_mod(c_all, w_ada[i], b_ada[i])
    mods_p = [m.reshape(bp, 1, d) for m in jnp.split(mod[:bp], 6, axis=1)]
    mods_s = [m.reshape(1, ns, d) for m in jnp.split(mod[bp:bp + ns], 6, axis=1)]

    xp2d = x_prompt.reshape(bp * L, d)
    xs2d = x_sample.reshape(ns, d)
    qb_p, kp, kb_p, vp, z_p, xbc_p, gate_p, dt_p, vt_p = _layer_in(
        xp2d, mods_p, jnp.arange(L), p, tm=256, rows_per_mod=L, with_vt=True)
    qb_s, ks, _, vs, z_s, xbc_s, gate_s, dt_s = _layer_in(
        xs2d, mods_s, jnp.full((ns,), past), p, tm=ns, rows_per_mod=ns, with_vt=False)

    pool = cache_k.shape[1]
    ck = cache_k[i].reshape(pool, PAGE_SIZE * N_HEADS, V_DIM)
    cv = cache_v[i].reshape(pool, PAGE_SIZE * N_HEADS, V_DIM)
    qh = qb_s.reshape(ns, N_HEADS, V_DIM)
    lane = jnp.arange(V_DIM)[None, None, :]
    q16 = jnp.concatenate([jnp.where(lane < HEAD_DIM, qh, 0), jnp.where(lane >= HEAD_DIM, qh, 0)], axis=1)
    padn = lambda t: jnp.concatenate(
        [t.reshape(ns, N_HEADS, V_DIM), jnp.zeros((ns, LANES - N_HEADS, V_DIM), F32)], axis=1)
    r3 = lambda t: t.reshape(bp, L, -1)
    o_p, o_s = _attention(r3(qb_p), r3(kb_p), vt_p, q16, padn(ks), padn(vs), ck, cv, page_table,
                          *lam_vecs, sw, lam_init, tq=512, pages=16, group=4)

    chunk = 128
    tri = jnp.tril(jnp.ones((chunk, chunk), F32)).astype(BF16)
    xbc3 = xbc_p.reshape(bp, L, CONV_DIM)
    ys_p, sp = _ssd_prompt(xbc3, z_p.reshape(bp, L, D_INNER), dt_p.reshape(bp, L, DT_PAD),
                           *mamba_params[:6], e01, tri, chunk)
    cp = xbc3[:, L - (CONV_W - 1):, :]
    ys_s, cs, ss = _ssd_step(xbc_s.reshape(ns, 1, CONV_DIM), z_s.reshape(ns, 1, D_INNER),
                             dt_s.reshape(ns, 1, DT_PAD), state_conv[i],
                             state_ssm[i].reshape(ns, D_INNER, D_STATE), *mamba_params)

    yp = _layer_out(xp2d, mods_p, o_p.reshape(bp * L, ATTN_WIDTH), ys_p.reshape(bp * L, D_INNER), gate_p, p,
                    tm=512, rows_per_mod=L)
    ys_ = _layer_out(xs2d, mods_s, o_s.reshape(ns, ATTN_WIDTH), ys_s.reshape(ns, D_INNER), gate_s, p,
                     tm=ns, rows_per_mod=ns)

    hk = (N_HEADS, 2 * HEAD_DIM)
    return (yp.reshape(bp, L, d), ys_.reshape(ns, 1, d),
            kp.reshape((1, bp, L) + hk), vp.reshape((1, bp, L) + hk),
            cp.reshape(1, bp, CONV_W - 1, CONV_DIM), sp.reshape(1, bp, SSM_HEADS, SSM_HEAD_DIM, D_STATE),
            ks.reshape((1, ns, 1) + hk), vs.reshape((1, ns, 1) + hk),
            cs.reshape(1, ns, CONV_W - 1, CONV_DIM), ss.reshape(1, ns, SSM_HEADS, SSM_HEAD_DIM, D_STATE))
```

```python
import functools
import math

import jax
import jax.numpy as jnp
import numpy as np
from jax import lax
from jax.experimental import pallas as pl
from jax.experimental.pallas import tpu as pltpu

F32 = jnp.float32
BF16 = jnp.bfloat16

D_MODEL = 1024
N_HEADS = 8
HEAD_DIM = 64
V_DIM = 2 * HEAD_DIM
QK_WIDTH = N_HEADS * 2 * HEAD_DIM
ATTN_WIDTH = N_HEADS * V_DIM
ROPE_DIM = HEAD_DIM // 4
ROPE_THETA = 500000.0
D_INNER = 2 * D_MODEL
SSM_HEAD_DIM = 64
SSM_HEADS = D_INNER // SSM_HEAD_DIM
N_GROUPS = 4
HEADS_PER_GROUP = SSM_HEADS // N_GROUPS
D_STATE = 128
CONV_W = 4
CONV_DIM = D_INNER + 2 * N_GROUPS * D_STATE
D_FF = -(-8 * D_MODEL // (3 * 256)) * 256
GATE_WIDTH = 2 * D_MODEL
PAGE_SIZE = 128
EPS = 1e-6

LANES = 128
SUBLANES = 8
DT_PAD = LANES
IN_WIDTH_PAD = 2 * QK_WIDTH + ATTN_WIDTH + D_INNER + CONV_DIM + GATE_WIDTH + DT_PAD
VMEM_LIMIT = 56 * 1024 * 1024

_OQ = 0
_OK = _OQ + QK_WIDTH
_OV = _OK + QK_WIDTH
_OZ = _OV + ATTN_WIDTH
_OX = _OZ + D_INNER
_OG = _OX + CONV_DIM
_OD = _OG + GATE_WIDTH


def _silu(x):
    return x * (1.0 / (1.0 + jnp.exp(-x)))


def _sigmoid(x):
    return 1.0 / (1.0 + jnp.exp(-x))


def _softplus(x):
    return jnp.maximum(x, 0.0) + jnp.log(1.0 + jnp.exp(-jnp.abs(x)))


def _split3(a):
    hi = a.astype(BF16)
    r1 = a - hi.astype(F32)
    mid = r1.astype(BF16)
    lo = (r1 - mid.astype(F32)).astype(BF16)
    return hi, mid, lo


def _dot(a, b):
    return jnp.dot(a, b, preferred_element_type=F32)


def _dot_nt(a, b):
    return lax.dot_general(a, b, (((1,), (1,)), ((), ())), preferred_element_type=F32)


def _exact_right(a, e01):
    hi, mid, lo = _split3(a)
    return _dot(hi, e01) + _dot(mid, e01) + _dot(lo, e01)


def _exact_left(t01, a):
    hi, mid, lo = _split3(a)
    return _dot(t01, hi) + _dot(t01, mid) + _dot(t01, lo)


def _ada_kernel(c_ref, w_ref, b_ref, o_ref):
    c = _silu(c_ref[...]).astype(BF16)
    o_ref[...] = _dot(c, w_ref[...].astype(BF16)) + b_ref[...]


def _ada_mod(c_all, w_ada, b_ada):
    rows = c_all.shape[0]
    tn = D_MODEL
    return pl.pallas_call(
        _ada_kernel,
        grid=(6 * D_MODEL // tn,),
        in_specs=[pl.BlockSpec((rows, D_MODEL), lambda j: (0, 0)),
                  pl.BlockSpec((D_MODEL, tn), lambda j: (0, j)),
                  pl.BlockSpec((1, tn), lambda j: (0, j))],
        out_specs=pl.BlockSpec((rows, tn), lambda j: (0, j)),
        out_shape=jax.ShapeDtypeStruct((rows, 6 * D_MODEL), F32),
        name="ada_mod",
    )(c_all, w_ada, b_ada.reshape(1, -1))


def _rope_cols(t, c_tab, s1_tab, s2_tab, scale):
    outs = []
    for cb in range(t.shape[1] // LANES):
        tc = t[:, cb * LANES:(cb + 1) * LANES]
        up = pltpu.roll(tc, LANES - ROPE_DIM // 2, axis=1)
        dn = pltpu.roll(tc, ROPE_DIM // 2, axis=1)
        r = tc * c_tab + up * s1_tab + dn * s2_tab
        outs.append(r * scale if scale != 1.0 else r)
    return outs


def _inproj_kernel(x_ref, sh_ref, sc_ref, nw_ref, w_ref, wg_ref, wd_ref, ct_ref, s1_ref, s2_ref,
                   q_ref, k_ref, kb_ref, v_ref, z_ref, xbc_ref, g_ref, dt_ref, *maybe_vt_ref):
    x = x_ref[...]
    ms = jnp.mean(x * x, axis=-1, keepdims=True)
    h = x * lax.rsqrt(ms + EPS) * nw_ref[...]
    h = h * (1.0 + sc_ref[0]) + sh_ref[0]
    hb = h.astype(BF16)

    def proj(lo, width):
        return _dot(hb, w_ref[:, lo:lo + width])

    ct, s1, s2 = ct_ref[...], s1_ref[...], s2_ref[...]
    q = _rope_cols(proj(_OQ, QK_WIDTH), ct, s1, s2, math.log2(math.e) / math.sqrt(HEAD_DIM))
    for cb, r in enumerate(q):
        q_ref[:, cb * LANES:(cb + 1) * LANES] = r.astype(BF16)
    k = _rope_cols(proj(_OK, QK_WIDTH), ct, s1, s2, 1.0)
    for cb, r in enumerate(k):
        k_ref[:, cb * LANES:(cb + 1) * LANES] = r
        kb_ref[:, cb * LANES:(cb + 1) * LANES] = r.astype(BF16)
    v = proj(_OV, ATTN_WIDTH)
    v_ref[...] = v
    for vt_ref in maybe_vt_ref:
        for cb in range(ATTN_WIDTH // LANES):
            vt_ref[0, cb * LANES:(cb + 1) * LANES, :] = v[:, cb * LANES:(cb + 1) * LANES].T.astype(BF16)
    z_ref[...] = proj(_OZ, D_INNER)
    xbc_ref[...] = proj(_OX, CONV_DIM)
    g_ref[...] = _dot(hb, wg_ref[...])
    dt_ref[...] = _dot(hb, wd_ref[...])


def _in_proj(x2d, sh, sc, nw, w_in, ctab, s1tab, s2tab, tm, rows_per_mod, with_vt):
    w_main, w_gate, w_dt = w_in
    wfull = lambda w: pl.BlockSpec(w.shape, lambda i: (0, 0), pipeline_mode=pl.Buffered(1))
    rows = x2d.shape[0]
    mod_rows = sh.shape[1]
    tiles_per_mod = rows_per_mod // tm
    row_blk = lambda width: pl.BlockSpec((tm, width), lambda i: (i, 0))
    mod_blk = pl.BlockSpec((1, mod_rows, D_MODEL), lambda i: (i // tiles_per_mod, 0, 0))
    assert ctab.shape[0] == rows_per_mod
    tab_blk = pl.BlockSpec((tm, LANES), lambda i: (i % tiles_per_mod, 0))
    outs = [(QK_WIDTH, BF16), (QK_WIDTH, F32), (QK_WIDTH, BF16), (ATTN_WIDTH, F32),
            (D_INNER, F32), (CONV_DIM, F32), (GATE_WIDTH, F32), (DT_PAD, F32)]
    out_specs = [row_blk(w) for w, _ in outs]
    out_shape = [jax.ShapeDtypeStruct((rows, w), dt) for w, dt in outs]
    if with_vt:
        out_specs.append(pl.BlockSpec((1, ATTN_WIDTH, tm), lambda i: (i, 0, 0)))
        out_shape.append(jax.ShapeDtypeStruct((rows // tm, ATTN_WIDTH, tm), BF16))
    return pl.pallas_call(
        _inproj_kernel,
        grid=(rows // tm,),
        in_specs=[row_blk(D_MODEL), mod_blk, mod_blk,
                  pl.BlockSpec((1, D_MODEL), lambda i: (0, 0)),
                  wfull(w_main), wfull(w_gate), wfull(w_dt),
                  tab_blk, tab_blk, tab_blk],
        out_specs=out_specs,
        out_shape=out_shape,
        compiler_params=pltpu.CompilerParams(dimension_semantics=("arbitrary",), vmem_limit_bytes=VMEM_LIMIT),
        name="in_proj",
    )(x2d, sh, sc, nw, w_main, w_gate, w_dt, ctab, s1tab, s2tab)


def _rope_tables(pos):
    half = ROPE_DIM // 2
    inv_freq = ROPE_THETA ** (-jnp.arange(half, dtype=F32) * 2.0 / ROPE_DIM)
    ang = pos.astype(F32)[:, None] * inv_freq[None, :]
    cos, sin = jnp.cos(ang), jnp.sin(ang)
    n = pos.shape[0]
    rest = HEAD_DIM - ROPE_DIM
    c64 = jnp.concatenate([cos, cos, jnp.ones((n, rest), F32)], axis=1)
    s1_64 = jnp.concatenate([-sin, jnp.zeros((n, half + rest), F32)], axis=1)
    s2_64 = jnp.concatenate([jnp.zeros((n, half), F32), sin, jnp.zeros((n, rest), F32)], axis=1)
    rep = LANES // HEAD_DIM
    return jnp.tile(c64, (1, rep)), jnp.tile(s1_64, (1, rep)), jnp.tile(s2_64, (1, rep))


def _lambda_value(lq1, lk1, lq2, lk2, lam_init):
    a = jnp.sum(lq1[...] * lk1[...], axis=1, keepdims=True)
    b = jnp.sum(lq2[...] * lk2[...], axis=1, keepdims=True)
    return jnp.exp(a) - jnp.exp(b) + lam_init


def _prompt_attn_part(qi, q_ref, k_ref, vt_ref, lam, sw_ref, o_ref,
                      qm_ref, sa_ref, sb_ref, m_ref, acc_ref, *, tq, tk, lam_init, between):
    assert tq == 2 * tk
    q = q_ref[0]
    lane = lax.broadcasted_iota(jnp.int32, q.shape, 1)
    zero = jnp.zeros_like(q)
    qm_ref[0] = jnp.where(lane < HEAD_DIM, q, zero)
    qm_ref[1] = jnp.where(lane >= HEAD_DIM, q, zero)
    m_ref[...] = jnp.full(m_ref.shape, -jnp.inf, F32)
    acc_ref[...] = jnp.zeros(acc_ref.shape, F32)

    def scores(ki, s_ref):
        k = k_ref[0, pl.ds(pl.multiple_of(ki * tk, tk), tk), :]
        for mp in range(2):
            s_ref[mp] = _dot_nt(k, qm_ref[mp])

    def absorb(ki, s_ref, diag):
        vt = jnp.concatenate([vt_ref[ki], jnp.ones((acc_ref.shape[1] - V_DIM, tk), BF16)], axis=0)
        for mp in range(2):
            st = s_ref[mp]
            if diag is not None:
                key = diag * tk + lax.broadcasted_iota(jnp.int32, st.shape, 0)
                qry = lax.broadcasted_iota(jnp.int32, st.shape, 1)
                st = jnp.where(key <= qry, st, -jnp.inf)
            m_prev = m_ref[mp]
            m_new = jnp.maximum(m_prev, jnp.max(st, axis=0, keepdims=True))
            alpha = jnp.exp2(m_prev - m_new)
            pt = jnp.exp2(st - m_new)
            acc_ref[mp] = alpha * acc_ref[mp] + _dot(vt, pt.astype(BF16))
            m_ref[mp] = m_new

    n_full = 2 * qi
    scores(0, sa_ref)

    def pair(t0):
        scores(t0 + 1, sb_ref)
        absorb(t0, sa_ref, None)
        scores(t0 + 2, sa_ref)
        absorb(t0 + 1, sb_ref, None)

    def body(j, carry):
        pair(4 * j)
        pair(4 * j + 2)
        return carry

    lax.fori_loop(0, lax.shift_right_logical(qi, 1), body, 0)

    @pl.when(lax.bitwise_and(qi, 1) == 1)
    def _():
        pair(n_full - 2)

    scores(n_full + 1, sb_ref)
    between()
    absorb(n_full, sa_ref, 0)
    absorb(n_full + 1, sb_ref, 1)

    o1 = acc_ref[0, 0:V_DIM, :] / acc_ref[0, V_DIM:V_DIM + 1, :]
    o2 = acc_ref[1, 0:V_DIM, :] / acc_ref[1, V_DIM:V_DIM + 1, :]
    ot = o1 - lam * o2
    ms = jnp.mean(ot * ot, axis=0, keepdims=True)
    ot = ot * lax.rsqrt(ms + EPS) * sw_ref[...] * (1.0 - lam_init)
    o_ref[0] = ot.T.astype(o_ref.dtype)


def _sample_attn_part(j, n_j, q_ref, kn_ref, vn_ref, k_refs, v_refs, lam, sw_ref, o_ref,
                      m_ref, l_ref, acc_ref, *, group, lam_init):
    pages = len(k_refs)
    nrow = 2 * N_HEADS
    q16 = q_ref[0]

    def head_mask(nkeys):
        row = lax.broadcasted_iota(jnp.int32, (nrow, nkeys), 0)
        col = lax.broadcasted_iota(jnp.int32, (nrow, nkeys), 1)
        return (row % N_HEADS) == (col % N_HEADS)

    def scores(ks, valid):
        return [jnp.where(valid, _dot(q16, k2d.astype(BF16).T), -jnp.inf) for k2d in ks]

    def absorb(ss, vs):
        m_blk = jnp.max(ss[0], axis=1, keepdims=True)
        for s in ss[1:]:
            m_blk = jnp.maximum(m_blk, jnp.max(s, axis=1, keepdims=True))
        m_prev = m_ref[...]
        m_new = jnp.maximum(m_prev, m_blk)
        alpha = jnp.exp2(m_prev - m_new)
        l_new = alpha * l_ref[...]
        acc = alpha * acc_ref[...]
        for s, v2d in zip(ss, vs):
            p = jnp.exp2(s - m_new)
            l_new = l_new + jnp.sum(p, axis=1, keepdims=True)
            acc = acc + _dot(p.astype(BF16), v2d)
        l_ref[...] = l_new
        acc_ref[...] = acc
        m_ref[...] = m_new

    def start():
        @pl.when(j == 0)
        def _():
            m_ref[...] = jnp.full(m_ref.shape, -jnp.inf, F32)
            l_ref[...] = jnp.zeros(l_ref.shape, F32)
            acc_ref[...] = jnp.zeros(acc_ref.shape, F32)
            col = lax.broadcasted_iota(jnp.int32, (nrow, LANES), 1)
            absorb(scores([kn_ref[0]], head_mask(LANES) & (col < N_HEADS)), [vn_ref[0].astype(BF16)])

    def absorb_pages():
        valid = head_mask(PAGE_SIZE * N_HEADS)
        bounds = list(range(0, pages + 1, group))
        ss = scores([r[0] for r in k_refs[0:group]], valid)
        for lo, hi in zip(bounds[:-1], bounds[1:]):
            nxt = scores([r[0] for r in k_refs[hi:hi + group]], valid) if hi < pages else None
            absorb(ss, [r[0].astype(BF16) for r in v_refs[lo:hi]])
            ss = nxt

    def finish():
        @pl.when(j == n_j - 1)
        def _():
            o1 = acc_ref[0:N_HEADS, :] / l_ref[0:N_HEADS, :]
            o2 = acc_ref[N_HEADS:nrow, :] / l_ref[N_HEADS:nrow, :]
            o = o1 - lam * o2
            ms = jnp.mean(o * o, axis=-1, keepdims=True)
            o_ref[0] = (o * lax.rsqrt(ms + EPS) * sw_ref[...] * (1.0 - lam_init)).astype(o_ref.dtype)

    return start, absorb_pages, finish


def _attn_kernel(pt_ref, q_ref, k_ref, vt_ref, qs_ref, kn_ref, vn_ref, *rest, pages, group, steps_per_seq,
                 tq, tk, lam_init):
    k_refs = rest[:pages]
    v_refs = rest[pages:2 * pages]
    (lq1, lk1, lq2, lk2, sw_col_ref, sw_row_ref, o_ref, os_ref,
     qm_ref, sa_ref, sb_ref, m_ref, acc_ref, ms_ref, ls_ref, accs_ref) = rest[2 * pages:]
    step = (pl.program_id(0) * pl.num_programs(1) + pl.program_id(1)) * pl.num_programs(2) + pl.program_id(2)
    lam = _lambda_value(lq1, lk1, lq2, lk2, lam_init)
    start, absorb_pages, finish = _sample_attn_part(
        lax.bitwise_and(step, steps_per_seq - 1), steps_per_seq, qs_ref, kn_ref, vn_ref, k_refs, v_refs, lam,
        sw_row_ref, os_ref, ms_ref, ls_ref, accs_ref, group=group, lam_init=lam_init)
    start()
    _prompt_attn_part(pl.program_id(2), q_ref, k_ref, vt_ref, lam, sw_col_ref, o_ref,
                      qm_ref, sa_ref, sb_ref, m_ref, acc_ref, tq=tq, tk=tk, lam_init=lam_init,
                      between=absorb_pages)
    finish()


def _attention(qb, kb, vt, q16, kn_pad, vn_pad, cache_k, cache_v, page_table, lq1, lk1, lq2, lk2, subln_w,
               lam_init, tq, pages, group):
    b, L, _ = qb.shape
    tk = vt.shape[2]
    nq = L // tq
    nseq, n_pages = page_table.shape
    assert n_pages % pages == 0 and pages % group == 0
    steps_per_seq = n_pages // pages
    assert b * N_HEADS * nq == nseq * steps_per_seq

    sps_log2 = steps_per_seq.bit_length() - 1
    assert steps_per_seq == 1 << sps_log2

    def seq_of(bi, h, qi):
        return lax.shift_right_logical((bi * N_HEADS + h) * nq + qi, sps_log2)

    def page_blk(i):
        def idx(bi, h, qi, pt):
            return (pt[((bi * N_HEADS + h) * nq + qi) * pages + i], 0, 0)
        return pl.BlockSpec((1, PAGE_SIZE * N_HEADS, V_DIM), idx)

    const = lambda shape: pl.BlockSpec(shape, lambda bi, h, qi, pt: (0, 0))
    per_seq = lambda rows: pl.BlockSpec((1, rows, V_DIM), lambda bi, h, qi, pt: (seq_of(bi, h, qi), 0, 0))
    q_blk = pl.BlockSpec((1, tq, V_DIM), lambda bi, h, qi, pt: (bi, qi, h))
    vec = const((1, HEAD_DIM))
    grid_spec = pltpu.PrefetchScalarGridSpec(
        num_scalar_prefetch=1,
        grid=(b, N_HEADS, nq),
        in_specs=[q_blk,
                  pl.BlockSpec((1, L, V_DIM), lambda bi, h, qi, pt: (bi, 0, h)),
                  pl.BlockSpec((L // tk, V_DIM, tk), lambda bi, h, qi, pt: (bi, h, 0)),
                  per_seq(2 * N_HEADS), per_seq(LANES), per_seq(LANES)]
                 + [page_blk(i) for i in range(pages)] + [page_blk(i) for i in range(pages)]
                 + [vec, vec, vec, vec, const((V_DIM, 1)), const((1, V_DIM))],
        out_specs=[q_blk, per_seq(N_HEADS)],
        scratch_shapes=[pltpu.VMEM((2, tq, V_DIM), BF16),
                        pltpu.VMEM((2, tk, tq), F32),
                        pltpu.VMEM((2, tk, tq), F32),
                        pltpu.VMEM((2, 1, tq), F32),
                        pltpu.VMEM((2, V_DIM + 16, tq), F32),
                        pltpu.VMEM((2 * N_HEADS, 1), F32),
                        pltpu.VMEM((2 * N_HEADS, 1), F32),
                        pltpu.VMEM((2 * N_HEADS, V_DIM), F32)],
    )
    return pl.pallas_call(
        functools.partial(_attn_kernel, pages=pages, group=group, steps_per_seq=steps_per_seq,
                          tq=tq, tk=tk, lam_init=lam_init),
        grid_spec=grid_spec,
        out_shape=[jax.ShapeDtypeStruct((b, L, ATTN_WIDTH), BF16),
                   jax.ShapeDtypeStruct((nseq, N_HEADS, V_DIM), BF16)],
        compiler_params=pltpu.CompilerParams(
            dimension_semantics=("arbitrary", "arbitrary", "arbitrary"), vmem_limit_bytes=VMEM_LIMIT),
        name="attn",
    )(page_table.reshape(-1), qb, kb, vt, q16, kn_pad, vn_pad, *([cache_k] * pages), *([cache_v] * pages),
      lq1, lk1, lq2, lk2, subln_w.reshape(V_DIM, 1), subln_w.reshape(1, V_DIM))


def _gate_and_group_norm(y, z, nw):
    y = y * _silu(z)
    gw = D_INNER // N_GROUPS
    outs = []
    for g in range(N_GROUPS):
        yg = y[:, g * gw:(g + 1) * gw]
        ms = jnp.mean(yg * yg, axis=-1, keepdims=True)
        outs.append(yg * lax.rsqrt(ms + EPS) * nw[:, g * gw:(g + 1) * gw])
    return outs


def _ssd_prompt_kernel(xbc_ref, z_ref, dt_ref, cw_ref, cb_ref, dtb_ref, alog_ref, dskip_ref, nw_ref,
                       e_ref, tri_ref, ys_ref, ssm_ref, ext_ref, st_ref, y_ref, *, chunk):
    c = pl.program_id(1)
    halo = SUBLANES

    @pl.when(c == 0)
    def _():
        ext_ref[0:halo, :] = jnp.zeros((halo, CONV_DIM), F32)
        st_ref[...] = jnp.zeros(st_ref.shape, F32)

    xt = xbc_ref[0]
    ext_ref[halo:halo + chunk, :] = xt
    acc = cb_ref[...] + ext_ref[halo - 3:halo - 3 + chunk, :] * cw_ref[0:1, :]
    for jj in range(1, CONV_W):
        acc = acc + ext_ref[halo - 3 + jj:halo - 3 + jj + chunk, :] * cw_ref[jj:jj + 1, :]
    u = _silu(acc)
    ext_ref[0:halo, :] = xt[chunk - halo:chunk, :]

    xs = u[:, :D_INNER]
    tri = tri_ref[...]
    e01 = e_ref[...]
    dt = _softplus(dt_ref[0] + dtb_ref[...])
    a = dt * (-jnp.exp(alog_ref[...]))
    a_cum = _exact_left(tri, a)
    a_cum_t = a_cum.T
    ea = _exact_right(jnp.exp(a_cum), e01)
    dt_exp = _exact_right(dt, e01)
    to_end = _exact_right(dt * jnp.exp(a_cum[chunk - 1:chunk, :] - a_cum), e01)
    xw = (xs * to_end).astype(BF16)
    xdt_b = (xs * dt_exp).astype(BF16)
    causal = lax.broadcasted_iota(jnp.int32, (chunk, chunk), 0) >= lax.broadcasted_iota(jnp.int32, (chunk, chunk), 1)
    lane = lax.broadcasted_iota(jnp.int32, (chunk, LANES), 1)
    gw = D_INNER // N_GROUPS

    for g in range(N_GROUPS):
        bg = u[:, D_INNER + g * D_STATE:D_INNER + (g + 1) * D_STATE]
        cg = u[:, D_INNER + (N_GROUPS + g) * D_STATE:D_INNER + (N_GROUPS + g + 1) * D_STATE]
        bg_b, cg_b = bg.astype(BF16), cg.astype(BF16)
        cb = _dot_nt(cg_b, bg_b)
        st_g = st_ref[:, g * gw:(g + 1) * gw]
        y_off = _dot(cg_b, st_g.astype(BF16))
        s_new = _dot(bg.T.astype(BF16), xw[:, g * gw:(g + 1) * gw])
        st_ref[:, g * gw:(g + 1) * gw] = st_g * ea[chunk - 1:chunk, g * gw:(g + 1) * gw] + s_new
        for pp in range(HEADS_PER_GROUP // 2):
            col0 = g * gw + pp * LANES
            xpair = xdt_b[:, col0:col0 + LANES]
            ys_pair = []
            for hh in range(2):
                r = g * HEADS_PER_GROUP + 2 * pp + hh
                seg = a_cum[:, r:r + 1] - a_cum_t[r:r + 1, :]
                decay = jnp.exp(jnp.where(causal, seg, -jnp.inf))
                ys_pair.append(_dot((cb * decay).astype(BF16), xpair))
            y_diag = jnp.where(lane < SSM_HEAD_DIM, ys_pair[0], ys_pair[1])
            y_ref[:, col0:col0 + LANES] = y_diag + ea[:, col0:col0 + LANES] * y_off[:, pp * LANES:(pp + 1) * LANES]

    y = y_ref[...] + dskip_ref[...] * xs
    for g, o in enumerate(_gate_and_group_norm(y, z_ref[0], nw_ref[...])):
        ys_ref[0, :, g * gw:(g + 1) * gw] = o.astype(ys_ref.dtype)

    @pl.when(c == pl.num_programs(1) - 1)
    def _():
        ssm_ref[0] = st_ref[...].T


def _ssd_prompt(xbc, z, dt, conv_w, conv_b, dtb_pad, alog_pad, dskip_exp, nw, e01, tri, chunk):
    b, L, _ = xbc.shape
    full = lambda shape: pl.BlockSpec(shape, lambda bi, c: tuple(0 for _ in shape))
    return pl.pallas_call(
        functools.partial(_ssd_prompt_kernel, chunk=chunk),
        grid=(b, L // chunk),
        in_specs=[pl.BlockSpec((1, chunk, CONV_DIM), lambda bi, c: (bi, c, 0)),
                  pl.BlockSpec((1, chunk, D_INNER), lambda bi, c: (bi, c, 0)),
                  pl.BlockSpec((1, chunk, DT_PAD), lambda bi, c: (bi, c, 0)),
                  full((CONV_W, CONV_DIM)), full((1, CONV_DIM)), full((1, DT_PAD)), full((1, DT_PAD)),
                  full((1, D_INNER)), full((1, D_INNER)), full((LANES, D_INNER)), full((chunk, chunk))],
        out_specs=[pl.BlockSpec((1, chunk, D_INNER), lambda bi, c: (bi, c, 0)),
                   pl.BlockSpec((1, D_INNER, D_STATE), lambda bi, c: (bi, 0, 0))],
        out_shape=[jax.ShapeDtypeStruct((b, L, D_INNER), BF16),
                   jax.ShapeDtypeStruct((b, D_INNER, D_STATE), F32)],
        scratch_shapes=[pltpu.VMEM((SUBLANES + chunk, CONV_DIM), F32),
                        pltpu.VMEM((D_STATE, D_INNER), F32),
                        pltpu.VMEM((chunk, D_INNER), F32)],
        compiler_params=pltpu.CompilerParams(
            dimension_semantics=("arbitrary", "arbitrary"), vmem_limit_bytes=VMEM_LIMIT),
        name="ssd_prompt",
    )(xbc, z, dt, conv_w, conv_b, dtb_pad, alog_pad, dskip_exp, nw, e01, tri)


def _ssd_step_kernel(xbc_ref, z_ref, dt_ref, conv_ref, ssm_ref, cw_ref, cb_ref, dtb_ref, alog_ref, dskip_ref,
                     nw_ref, e_ref, ys_ref, conv_out_ref, ssm_out_ref):
    xn = xbc_ref[0]
    acc = cb_ref[...] + xn * cw_ref[CONV_W - 1:CONV_W, :]
    for jj in range(CONV_W - 1):
        acc = acc + conv_ref[0, jj:jj + 1, :] * cw_ref[jj:jj + 1, :]
        if jj > 0:
            conv_out_ref[0, jj - 1:jj, :] = conv_ref[0, jj:jj + 1, :]
    u = _silu(acc)
    conv_out_ref[0, CONV_W - 2:CONV_W - 1, :] = xn

    xs = u[:, :D_INNER]
    e01 = e_ref[...]
    dt = _softplus(dt_ref[0] + dtb_ref[...])
    d_a = jnp.exp(dt * (-jnp.exp(alog_ref[...])))
    rows8 = lax.broadcasted_iota(jnp.int32, (SUBLANES, DT_PAD), 0)
    two = jnp.where(rows8 == 0, dt, jnp.where(rows8 == 1, d_a, 0.0))
    two_exp = _exact_right(two, e01)
    xdt = xs * two_exp[0:1, :]
    rows = lax.broadcasted_iota(jnp.int32, (LANES, D_INNER), 0)
    stack = jnp.where(rows == 0, xdt, jnp.where(rows == 1, two_exp[1:2, :], 0.0))
    rows_c = lax.broadcasted_iota(jnp.int32, (SUBLANES, D_STATE), 0)
    c8 = jnp.zeros((SUBLANES, D_STATE), F32)
    for g in range(N_GROUPS):
        c_g = u[:, D_INNER + (N_GROUPS + g) * D_STATE:D_INNER + (N_GROUPS + g + 1) * D_STATE]
        c8 = jnp.where(rows_c == g, c_g, c8)
    c8 = c8.astype(BF16)
    blocks_per_group = D_INNER // N_GROUPS // LANES
    y_blocks = []
    for cbk in range(D_INNER // LANES):
        g = cbk // blocks_per_group
        cols = stack[:, cbk * LANES:(cbk + 1) * LANES].T
        b_row = u[:, D_INNER + g * D_STATE:D_INNER + (g + 1) * D_STATE]
        s_old = ssm_ref[0, cbk * LANES:(cbk + 1) * LANES, :]
        s_new = cols[:, 1:2] * s_old + cols[:, 0:1] * b_row
        ssm_out_ref[0, cbk * LANES:(cbk + 1) * LANES, :] = s_new
        yg = _dot_nt(c8, s_new.astype(BF16))
        y_blocks.append(yg[g:g + 1, :])
    y = jnp.concatenate(y_blocks, axis=1) + dskip_ref[...] * xs
    gw = D_INNER // N_GROUPS
    for g, o in enumerate(_gate_and_group_norm(y, z_ref[0], nw_ref[...])):
        ys_ref[0, :, g * gw:(g + 1) * gw] = o.astype(ys_ref.dtype)


def _ssd_step(xbc, z, dt, state_conv, state_ssm, conv_w, conv_b, dtb_pad, alog_pad, dskip_exp, nw, e01):
    nseq = xbc.shape[0]
    full = lambda shape: pl.BlockSpec(shape, lambda s: tuple(0 for _ in shape))
    per = lambda shape: pl.BlockSpec((1,) + shape, lambda s: (s, 0, 0))
    return pl.pallas_call(
        _ssd_step_kernel,
        grid=(nseq,),
        in_specs=[per((1, CONV_DIM)), per((1, D_INNER)), per((1, DT_PAD)), per((CONV_W - 1, CONV_DIM)),
                  per((D_INNER, D_STATE)),
                  full((CONV_W, CONV_DIM)), full((1, CONV_DIM)), full((1, DT_PAD)), full((1, DT_PAD)),
                  full((1, D_INNER)), full((1, D_INNER)), full((LANES, D_INNER))],
        out_specs=[per((1, D_INNER)), per((CONV_W - 1, CONV_DIM)), per((D_INNER, D_STATE))],
        out_shape=[jax.ShapeDtypeStruct((nseq, 1, D_INNER), BF16),
                   jax.ShapeDtypeStruct((nseq, CONV_W - 1, CONV_DIM), F32),
                   jax.ShapeDtypeStruct((nseq, D_INNER, D_STATE), F32)],
        compiler_params=pltpu.CompilerParams(dimension_semantics=("arbitrary",), vmem_limit_bytes=VMEM_LIMIT),
        name="ssd_step",
    )(xbc, z, dt, state_conv, state_ssm, conv_w, conv_b, dtb_pad, alog_pad, dskip_exp, nw, e01)


def _merge_kernel(x_ref, o_ref, ys_ref, g_ref, g1_ref, wa_ref, wb_ref, wo_ref, x1_ref):
    ya = _dot(o_ref[...], wa_ref[...])
    yb = _dot(ys_ref[...], wb_ref[...])
    gates = _sigmoid(g_ref[...])
    mix = gates[:, :D_MODEL] * ya + gates[:, D_MODEL:] * yb
    mixed = _dot(mix.astype(BF16), wo_ref[...])
    x1_ref[...] = x_ref[...] + g1_ref[0] * mixed


def _merge(x2d, o2d, ys2d, gate2d, g1, wa, wb, wo, tm, rows_per_mod):
    rows = x2d.shape[0]
    mod_rows = g1.shape[1]
    tiles_per_mod = rows_per_mod // tm
    row_blk = lambda width: pl.BlockSpec((tm, width), lambda i: (i, 0))
    wfull = lambda shape: pl.BlockSpec(shape, lambda i: (0, 0), pipeline_mode=pl.Buffered(1))
    return pl.pallas_call(
        _merge_kernel,
        grid=(rows // tm,),
        in_specs=[row_blk(D_MODEL), row_blk(ATTN_WIDTH), row_blk(D_INNER), row_blk(GATE_WIDTH),
                  pl.BlockSpec((1, mod_rows, D_MODEL), lambda i: (i // tiles_per_mod, 0, 0)),
                  wfull((ATTN_WIDTH, D_MODEL)), wfull((D_INNER, D_MODEL)), wfull((D_MODEL, D_MODEL))],
        out_specs=row_blk(D_MODEL),
        out_shape=jax.ShapeDtypeStruct((rows, D_MODEL), F32),
        compiler_params=pltpu.CompilerParams(dimension_semantics=("arbitrary",), vmem_limit_bytes=VMEM_LIMIT),
        name="merge",
    )(x2d, o2d, ys2d, gate2d, g1, wa, wb, wo)


def _ffn_kernel(x_ref, sh_ref, sc_ref, g2_ref, nw_ref, fw_ref, wg_ref, wu_ref, wd_ref, y_ref):
    x = x_ref[...]
    ms = jnp.mean(x * x, axis=-1, keepdims=True)
    h = x * lax.rsqrt(ms + EPS) * nw_ref[...]
    hb = (h * (1.0 + sc_ref[0]) + sh_ref[0]).astype(BF16)
    ff = _silu(_dot(hb, wg_ref[...])) * _dot(hb, wu_ref[...])
    x2 = x + g2_ref[0] * _dot(ff.astype(BF16), wd_ref[...])
    ms2 = jnp.mean(x2 * x2, axis=-1, keepdims=True)
    y_ref[...] = x2 * lax.rsqrt(ms2 + EPS) * fw_ref[...]


def _ffn(x2d, sh, sc, g2, nw, fw, wg, wu, wd, tm, rows_per_mod):
    rows = x2d.shape[0]
    mod_rows = sh.shape[1]
    tiles_per_mod = rows_per_mod // tm
    row_blk = pl.BlockSpec((tm, D_MODEL), lambda i: (i, 0))
    mod_blk = pl.BlockSpec((1, mod_rows, D_MODEL), lambda i: (i // tiles_per_mod, 0, 0))
    vec = pl.BlockSpec((1, D_MODEL), lambda i: (0, 0))
    wfull = lambda shape: pl.BlockSpec(shape, lambda i: (0, 0), pipeline_mode=pl.Buffered(1))
    return pl.pallas_call(
        _ffn_kernel,
        grid=(rows // tm,),
        in_specs=[row_blk, mod_blk, mod_blk, mod_blk, vec, vec,
                  wfull((D_MODEL, D_FF)), wfull((D_MODEL, D_FF)), wfull((D_FF, D_MODEL))],
        out_specs=row_blk,
        out_shape=jax.ShapeDtypeStruct((rows, D_MODEL), F32),
        compiler_params=pltpu.CompilerParams(dimension_semantics=("arbitrary",), vmem_limit_bytes=VMEM_LIMIT),
        name="ffn",
    )(x2d, sh, sc, g2, nw, fw, wg, wu, wd)


def _layer_in(x2d, mods, pos, p, tm, rows_per_mod, with_vt):
    sh1, sc1 = mods[0], mods[1]
    ct, s1, s2 = _rope_tables(pos)
    return _in_proj(x2d, sh1, sc1, p['norm1_w'], p['w_in'], ct, s1, s2, tm, rows_per_mod, with_vt)


def _layer_out(x2d, mods, o, ys, gate, p, tm, rows_per_mod):
    _, _, g1, sh2, sc2, g2 = mods
    x1 = _merge(x2d, o, ys, gate, g1, p['w_branch_attn'], p['w_branch_ssm'], p['w_out'], tm, rows_per_mod)
    return _ffn(x1, sh2, sc2, g2, p['norm2_w'], p['final_norm_w'], p['w_ffn_gate'], p['w_ffn_up'],
                p['w_ffn_down'], tm, rows_per_mod)


def kernel(x_prompt, x_sample, c_prompt, c_sample, cache_k, cache_v, page_table, state_conv, state_ssm, w_ada, b_ada, norm1_w, w_in, lambda_q1, lambda_k1, lambda_q2, lambda_k2, subln_w, conv_w, conv_b, dt_bias, a_log, d_skip, ssm_norm_w, w_branch_attn, w_branch_ssm, w_out, norm2_w, w_ffn_gate, w_ffn_up, w_ffn_down, final_norm_w):
    depth = w_in.shape[0]
    assert depth == 1
    bp, L, d = x_prompt.shape
    ns, Ld, _ = x_sample.shape
    assert Ld == 1 and d == D_MODEL
    n_pages = page_table.shape[1]
    past = n_pages * PAGE_SIZE
    lam_init = 0.8 - 0.6 * math.exp(-0.3 * 0)
    i = 0

    w_main = w_in[i][:, :_OG].astype(BF16)
    w_dt = jnp.concatenate([w_in[i][:, _OG:_OG + SSM_HEADS].astype(BF16),
                            jnp.zeros((D_MODEL, DT_PAD - SSM_HEADS), BF16)], axis=1)
    w_gate = w_in[i][:, _OG + SSM_HEADS:].astype(BF16)
    w_in_r = (w_main, w_gate, w_dt)
    pad_heads = lambda t: jnp.concatenate([t, jnp.zeros((DT_PAD - SSM_HEADS,), F32)]).reshape(1, DT_PAD)
    e01 = (jnp.arange(LANES)[:, None] == (jnp.arange(D_INNER)[None, :] // SSM_HEAD_DIM)).astype(BF16)
    p = {
        'norm1_w': norm1_w[i].reshape(1, -1), 'w_in': w_in_r,
        'w_branch_attn': w_branch_attn[i].astype(BF16), 'w_branch_ssm': w_branch_ssm[i].astype(BF16),
        'w_out': w_out[i].astype(BF16), 'norm2_w': norm2_w[i].reshape(1, -1),
        'final_norm_w': final_norm_w.reshape(1, -1),
        'w_ffn_gate': w_ffn_gate[i].astype(BF16), 'w_ffn_up': w_ffn_up[i].astype(BF16),
        'w_ffn_down': w_ffn_down[i].astype(BF16),
    }
    lam_vecs = [t[i].reshape(1, HEAD_DIM) for t in (lambda_q1, lambda_k1, lambda_q2, lambda_k2)]
    sw = subln_w[i].reshape(1, V_DIM)
    mamba_params = (conv_w[i], conv_b[i].reshape(1, -1), pad_heads(dt_bias[i]), pad_heads(a_log[i]),
                    jnp.repeat(d_skip[i], SSM_HEAD_DIM).reshape(1, D_INNER), ssm_norm_w[i].reshape(1, -1), e01)

    n_c = bp + ns
    c_rows = -(-n_c // SUBLANES) * SUBLANES
    c_all = jnp.concatenate([c_prompt, c_sample, jnp.zeros((c_rows - n_c, d), F32)], axis=0)
    mod = _ada_mod(c_all, w_ada[i], b_ada[i])
    mods_p = [m.reshape(bp, 1, d) for m in jnp.split(mod[:bp], 6, axis=1)]
    mods_s = [m.reshape(1, ns, d) for m in jnp.split(mod[bp:bp + ns], 6, axis=1)]

    xp2d = x_prompt.reshape(bp * L, d)
    xs2d = x_sample.reshape(ns, d)
    qb_p, kp, kb_p, vp, z_p, xbc_p, gate_p, dt_p, vt_p = _layer_in(
        xp2d, mods_p, jnp.arange(L), p, tm=256, rows_per_mod=L, with_vt=True)
    qb_s, ks, _, vs, z_s, xbc_s, gate_s, dt_s = _layer_in(
        xs2d, mods_s, jnp.full((ns,), past), p, tm=ns, rows_per_mod=ns, with_vt=False)

    pool = cache_k.shape[1]
    ck = cache_k[i].reshape(pool, PAGE_SIZE * N_HEADS, V_DIM)
    cv = cache_v[i].reshape(pool, PAGE_SIZE * N_HEADS, V_DIM)
    qh = qb_s.reshape(ns, N_HEADS, V_DIM)
    lane = jnp.arange(V_DIM)[None, None, :]
    q16 = jnp.concatenate([jnp.where(lane < HEAD_DIM, qh, 0), jnp.where(lane >= HEAD_DIM, qh, 0)], axis=1)
    padn = lambda t: jnp.concatenate(
        [t.reshape(ns, N_HEADS, V_DIM), jnp.zeros((ns, LANES - N_HEADS, V_DIM), F32)], axis=1)
    r3 = lambda t: t.reshape(bp, L, -1)
    o_p, o_s = _attention(r3(qb_p), r3(kb_p), vt_p, q16, padn(ks), padn(vs), ck, cv, page_table,
                          *lam_vecs, sw, lam_init, tq=512, pages=16, group=8)

    chunk = 128
    tri = jnp.tril(jnp.ones((chunk, chunk), F32)).astype(BF16)
    xbc3 = xbc_p.reshape(bp, L, CONV_DIM)
    ys_p, sp = _ssd_prompt(xbc3, z_p.reshape(bp, L, D_INNER), dt_p.reshape(bp, L, DT_PAD),
                           *mamba_params[:6], e01, tri, chunk)
    cp = xbc3[:, L - (CONV_W - 1):, :]
    ys_s, cs, ss = _ssd_step(xbc_s.reshape(ns, 1, CONV_DIM), z_s.reshape(ns, 1, D_INNER),
                             dt_s.reshape(ns, 1, DT_PAD), state_conv[i],
                             state_ssm[i].reshape(ns, D_INNER, D_STATE), *mamba_params)

    yp = _layer_out(xp2d, mods_p, o_p.reshape(bp * L, ATTN_WIDTH), ys_p.reshape(bp * L, D_INNER), gate_p, p,
                    tm=512, rows_per_mod=L)
    ys_ = _layer_out(xs2d, mods_s, o_s.reshape(ns, ATTN_WIDTH), ys_s.reshape(ns, D_INNER), gate_s, p,
                     tm=ns, rows_per_mod=ns)

    hk = (N_HEADS, 2 * HEAD_DIM)
    return (yp.reshape(bp, L, d), ys_.reshape(ns, 1, d),
            kp.reshape((1, bp, L) + hk), vp.reshape((1, bp, L) + hk),
            cp.reshape(1, bp, CONV_W - 1, CONV_DIM), sp.reshape(1, bp, SSM_HEADS, SSM_HEAD_DIM, D_STATE),
            ks.reshape((1, ns, 1) + hk), vs.reshape((1, ns, 1) + hk),
            cs.reshape(1, ns, CONV_W - 1, CONV_DIM), ss.reshape(1, ns, SSM_HEADS, SSM_HEAD_DIM, D_STATE))
```
